```python
import math
import numpy as np
import jax
import jax.numpy as jnp
from jax import lax

D_MODEL = 2048
BATCH = 4
SEQ = 2048
DEPTH = 2

HEAD_DIM = 64
N_BRANCH = 4
MIX_HEADS = D_MODEL // (N_BRANCH * HEAD_DIM)
BRANCH_WIDTH = MIX_HEADS * HEAD_DIM
Q_BLOCK = 128
ROPE_THETA = 500000.0
PARTIAL_ROT = HEAD_DIM // 4
RMS_EPS = 1e-6
NEG = -1e30

MLA_Q_LORA = D_MODEL // 4
MLA_KV_LORA = D_MODEL // 8
MLA_NOPE = HEAD_DIM
MLA_ROPE = HEAD_DIM // 2
MLA_V = HEAD_DIM

NSA_KV_HEADS = 2
NSA_KV_WIDTH = NSA_KV_HEADS * HEAD_DIM
NSA_CMP_LEN = 32
NSA_CMP_STRIDE = 16
NSA_CMP_HIDDEN = 256
NSA_SEL_LEN = 64
NSA_SEL_TOP = 16
NSA_WINDOW = 512
NSA_FORCE_BONUS = 1e3

PEER_HEADS = 8
PEER_KEYS = 128
PEER_EXPERTS = PEER_KEYS * PEER_KEYS
PEER_TOPK = 16
PEER_QDIM = 256
PEER_CHUNK = 128

IN_SPLITS = (
    BRANCH_WIDTH, BRANCH_WIDTH, BRANCH_WIDTH, MIX_HEADS,
    MLA_Q_LORA, MLA_KV_LORA, MLA_ROPE,
    BRANCH_WIDTH, NSA_KV_WIDTH, NSA_KV_WIDTH, NSA_KV_WIDTH,
    NSA_KV_WIDTH, NSA_KV_WIDTH, NSA_KV_WIDTH, 3 * MIX_HEADS,
    BRANCH_WIDTH, BRANCH_WIDTH, BRANCH_WIDTH,
    N_BRANCH * D_MODEL,
)
IN_WIDTH = sum(IN_SPLITS)

kernel_name = 'hybrid_fox_mla_nsa_stickbreak_peer'


def rms_norm(x, g):
    xf = x.astype(jnp.float32)
    y = xf * lax.rsqrt(jnp.mean(xf * xf, axis=-1, keepdims=True) + RMS_EPS)
    return (y * g.astype(jnp.float32)).astype(x.dtype)


def rope(x, pos, rot_dim):
    half = rot_dim // 2
    inv = jnp.exp(-math.log(ROPE_THETA) * jnp.arange(half, dtype=jnp.float32) / half)
    ang = pos.astype(jnp.float32)[:, None] * inv[None, :]
    cos = jnp.cos(ang)[:, None, :]
    sin = jnp.sin(ang)[:, None, :]
    xr = x[..., :rot_dim].astype(jnp.float32)
    x1, x2 = xr[..., :half], xr[..., half:]
    rot = jnp.concatenate([x1 * cos - x2 * sin, x1 * sin + x2 * cos], axis=-1)
    return jnp.concatenate([rot.astype(x.dtype), x[..., rot_dim:]], axis=-1)


def heads(t, n):
    return t.reshape(t.shape[:-1] + (n, -1))


def sweep_query_blocks(fn, S):
    out = lax.map(fn, jnp.arange(S // Q_BLOCK) * Q_BLOCK)
    out = jnp.moveaxis(out, 0, 1)
    return out.reshape((out.shape[0], S) + out.shape[3:])


def fox_attention(q, k, v, log_f):
    B, S, H, Dh = q.shape
    F = jnp.swapaxes(jnp.cumsum(log_f, axis=1), 1, 2)
    scale = Dh ** -0.5
    kpos = jnp.arange(S)

    def block(start):
        qb = lax.dynamic_slice_in_dim(q, start, Q_BLOCK, 1)
        Fq = lax.dynamic_slice_in_dim(F, start, Q_BLOCK, 2)
        s = jnp.einsum('bqhd,bkhd->bhqk', qb, k, preferred_element_type=jnp.float32) * scale
        s = s + Fq[..., :, None] - F[..., None, :]
        qpos = start + jnp.arange(Q_BLOCK)
        mask = kpos[None, :] <= qpos[:, None]
        p = jax.nn.softmax(jnp.where(mask, s, NEG), axis=-1)
        return jnp.einsum('bhqk,bkhd->bqhd', p.astype(v.dtype), v)

    return sweep_query_blocks(block, S)


def mla_attention(c_q, c_kv, k_r, g_q, g_kv, w_uq, w_ukv, pos):
    B, S, _ = c_q.shape
    q = (rms_norm(c_q, g_q) @ w_uq).reshape(B, S, MIX_HEADS, MLA_NOPE + MLA_ROPE)
    kv = (rms_norm(c_kv, g_kv) @ w_ukv).reshape(B, S, MIX_HEADS, MLA_NOPE + MLA_V)
    q_nope = q[..., :MLA_NOPE]
    q_rope = rope(q[..., MLA_NOPE:], pos, MLA_ROPE)
    k_nope, v = kv[..., :MLA_NOPE], kv[..., MLA_NOPE:]
    k_rope = rope(k_r[:, :, None, :], pos, MLA_ROPE)[:, :, 0]
    scale = (MLA_NOPE + MLA_ROPE) ** -0.5
    kpos = jnp.arange(S)

    def block(start):
        qn = lax.dynamic_slice_in_dim(q_nope, start, Q_BLOCK, 1)
        qr = lax.dynamic_slice_in_dim(q_rope, start, Q_BLOCK, 1)
        s = (jnp.einsum('bqhd,bkhd->bhqk', qn, k_nope, preferred_element_type=jnp.float32)
             + jnp.einsum('bqhd,bkd->bhqk', qr, k_rope, preferred_element_type=jnp.float32)) * scale
        qpos = start + jnp.arange(Q_BLOCK)
        mask = kpos[None, :] <= qpos[:, None]
        p = jax.nn.softmax(jnp.where(mask, s, NEG), axis=-1)
        return jnp.einsum('bhqk,bkhd->bqhd', p.astype(v.dtype), v)

    return sweep_query_blocks(block, S)


def compress_blocks(t, tok_idx, pos_emb, w1, w2):
    B, S, G, Dh = t.shape
    n_cmp, L = tok_idx.shape
    blocks = jnp.take(t, tok_idx, axis=1) + pos_emb[:, None, :]
    blocks = jnp.swapaxes(blocks, 2, 3).reshape(B, n_cmp, G, L * Dh)
    return jax.nn.gelu(blocks @ w1, approximate=False) @ w2


def nsa_attention(q, k_c, v_c, k_s, v_s, k_w, v_w, gates, pos,
                  kc_pos, kc_w1, kc_w2, vc_pos, vc_w1, vc_w2):
    B, S, H, Dh = q.shape
    G = k_c.shape[2]
    R = H // G
    scale = Dh ** -0.5
    q = rope(q, pos, PARTIAL_ROT)
    k_s = rope(k_s, pos, PARTIAL_ROT)
    k_w = rope(k_w, pos, PARTIAL_ROT)
    qg = q.reshape(B, S, G, R, Dh)

    n_cmp = (S - NSA_CMP_LEN) // NSA_CMP_STRIDE + 1
    cmp_start = jnp.arange(n_cmp) * NSA_CMP_STRIDE
    tok_idx = cmp_start[:, None] + jnp.arange(NSA_CMP_LEN)[None, :]
    cmp_end = cmp_start + NSA_CMP_LEN - 1
    kc = rope(compress_blocks(k_c, tok_idx, kc_pos, kc_w1, kc_w2), cmp_end, PARTIAL_ROT)
    vc = compress_blocks(v_c, tok_idx, vc_pos, vc_w1, vc_w2)
    s = jnp.einsum('bsgrd,bcgd->bgrsc', qg, kc, preferred_element_type=jnp.float32) * scale
    cmask = cmp_end[None, :] <= pos[:, None]
    p_cmp = jax.nn.softmax(jnp.where(cmask, s, NEG), axis=-1) * cmask
    o_cmp = jnp.einsum('bgrsc,bcgd->bsgrd', p_cmp.astype(vc.dtype), vc).reshape(B, S, H, Dh)

    n_sel = S // NSA_SEL_LEN
    sel_lo = jnp.arange(n_sel) * NSA_SEL_LEN
    cover = jnp.clip(jnp.minimum(cmp_start[:, None] + NSA_CMP_LEN, sel_lo[None, :] + NSA_SEL_LEN)
                     - jnp.maximum(cmp_start[:, None], sel_lo[None, :]), 0, None)
    cmp_to_sel = cover.astype(jnp.float32) / NSA_CMP_LEN
    imp = jnp.einsum('bgrsc,cj->bsgj', p_cmp, cmp_to_sel)
    blk = jnp.arange(n_sel)[None, :]
    cur = (pos // NSA_SEL_LEN)[:, None]
    forced = ((blk == 0) | (blk == cur) | (blk == cur - 1)).astype(jnp.float32)
    valid = blk * NSA_SEL_LEN <= pos[:, None]
    score = jnp.where(valid[None, :, None, :], imp + NSA_FORCE_BONUS * forced[None, :, None, :], NEG)
    n_top = min(NSA_SEL_TOP, n_sel)
    sel_score, sel_idx = lax.top_k(score, n_top)
    sel_ok = sel_score > 0.5 * NEG

    kb = jnp.moveaxis(k_s.reshape(B, n_sel, NSA_SEL_LEN, G, Dh), 3, 1)
    vb = jnp.moveaxis(v_s.reshape(B, n_sel, NSA_SEL_LEN, G, Dh), 3, 1)
    gather = jax.vmap(jax.vmap(lambda tbl, ix: tbl[ix]))

    def slc_block(start):
        qb = lax.dynamic_slice_in_dim(qg, start, Q_BLOCK, 1)
        ix = jnp.swapaxes(lax.dynamic_slice_in_dim(sel_idx, start, Q_BLOCK, 1), 1, 2)
        ok = jnp.swapaxes(lax.dynamic_slice_in_dim(sel_ok, start, Q_BLOCK, 1), 1, 2)
        kg = gather(kb, ix)
        vg = gather(vb, ix)
        s = jnp.einsum('bqgrd,bgqnld->bgrqnl', qb, kg, preferred_element_type=jnp.float32) * scale
        qpos = start + jnp.arange(Q_BLOCK)
        kpos = ix[..., None] * NSA_SEL_LEN + jnp.arange(NSA_SEL_LEN)
        mask = ((kpos <= qpos[:, None, None]) & ok[..., None])[:, :, None]
        s = jnp.where(mask, s, NEG).reshape(B, G, R, Q_BLOCK, -1)
        p = jax.nn.softmax(s, axis=-1)
        o = jnp.einsum('bgrqm,bgqmd->bqgrd', p.astype(vg.dtype), vg.reshape(B, G, Q_BLOCK, -1, Dh))
        return o.reshape(B, Q_BLOCK, H, Dh)

    o_slc = sweep_query_blocks(slc_block, S)

    pad = ((0, 0), (NSA_WINDOW, 0), (0, 0), (0, 0))
    kwp = jnp.pad(k_w, pad)
    vwp = jnp.pad(v_w, pad)

    def win_block(start):
        qb = lax.dynamic_slice_in_dim(qg, start, Q_BLOCK, 1)
        kband = lax.dynamic_slice_in_dim(kwp, start, Q_BLOCK + NSA_WINDOW, 1)
        vband = lax.dynamic_slice_in_dim(vwp, start, Q_BLOCK + NSA_WINDOW, 1)
        s = jnp.einsum('bqgrd,bkgd->bgrqk', qb, kband, preferred_element_type=jnp.float32) * scale
        qpos = start + jnp.arange(Q_BLOCK)
        kpos = start - NSA_WINDOW + jnp.arange(Q_BLOCK + NSA_WINDOW)
        d = qpos[:, None] - kpos[None, :]
        mask = (d >= 0) & (d < NSA_WINDOW) & (kpos[None, :] >= 0)
        p = jax.nn.softmax(jnp.where(mask, s, NEG), axis=-1)
        o = jnp.einsum('bgrqk,bkgd->bqgrd', p.astype(vband.dtype), vband)
        return o.reshape(B, Q_BLOCK, H, Dh)

    o_win = sweep_query_blocks(win_block, S)
    return gates[..., 0:1] * o_cmp + gates[..., 1:2] * o_slc + gates[..., 2:3] * o_win


def stick_breaking_attention(q, k, v):
    B, S, H, Dh = q.shape
    scale = Dh ** -0.5
    kpos = jnp.arange(S)

    def block(start):
        qb = lax.dynamic_slice_in_dim(q, start, Q_BLOCK, 1)
        z = jnp.einsum('bqhd,bkhd->bhqk', qb, k, preferred_element_type=jnp.float32) * scale
        qpos = start + jnp.arange(Q_BLOCK)
        mask = kpos[None, :] < qpos[:, None]
        log_beta = jax.nn.log_sigmoid(z)
        log_1m = jnp.where(mask, jax.nn.log_sigmoid(-z), 0.0)
        tail = lax.cumsum(log_1m, axis=3, reverse=True) - log_1m
        A = jnp.where(mask, jnp.exp(log_beta + tail), 0.0)
        return jnp.einsum('bhqk,bkhd->bqhd', A.astype(v.dtype), v)

    return sweep_query_blocks(block, S)


def peer_ffn(x, w_q, sub_k1, sub_k2, u_tab, v_tab):
    B, S, D = x.shape
    T = B * S
    xt = x.reshape(T, D)
    q = (xt @ w_q).reshape(T, PEER_HEADS, PEER_QDIM)
    half = PEER_QDIM // 2
    s1 = jnp.einsum('thd,hnd->thn', q[..., :half], sub_k1, preferred_element_type=jnp.float32)
    s2 = jnp.einsum('thd,hnd->thn', q[..., half:], sub_k2, preferred_element_type=jnp.float32)
    v1, i1 = lax.top_k(s1, PEER_TOPK)
    v2, i2 = lax.top_k(s2, PEER_TOPK)
    cand = (v1[..., :, None] + v2[..., None, :]).reshape(T, PEER_HEADS, PEER_TOPK * PEER_TOPK)
    cidx = (i1[..., :, None] * PEER_KEYS + i2[..., None, :]).reshape(T, PEER_HEADS, PEER_TOPK * PEER_TOPK)
    top_s, top_pos = lax.top_k(cand, PEER_TOPK)
    eidx = jnp.take_along_axis(cidx, top_pos, axis=-1)
    g = jax.nn.softmax(top_s, axis=-1)
    n_chunks = T // PEER_CHUNK
    xc = xt.reshape(n_chunks, PEER_CHUNK, D)
    ec = eidx.reshape(n_chunks, PEER_CHUNK, PEER_HEADS * PEER_TOPK)
    gc = g.reshape(n_chunks, PEER_CHUNK, PEER_HEADS * PEER_TOPK)

    def chunk(args):
        xb, eb, gb = args
        u = u_tab[eb]
        vv = v_tab[eb]
        a = jax.nn.gelu(jnp.einsum('cd,ced->ce', xb, u, preferred_element_type=jnp.float32),
                        approximate=False)
        return jnp.einsum('ce,ced->cd', (gb * a).astype(x.dtype), vv)

    return lax.map(chunk, (xc, ec, gc)).reshape(B, S, D)


def setup_inputs(seed: int = 0) -> dict:
    key = jax.random.key(seed)
    ks = jax.random.split(key, 24)
    L, D = DEPTH, D_MODEL
    f32 = jnp.float32

    def nrm(k, shape, fan_in):
        return jax.random.normal(k, shape, f32) * fan_in ** -0.5

    def gain(k, shape):
        return 1.0 + 0.02 * jax.random.normal(k, shape, f32)

    cmp_in = NSA_CMP_LEN * HEAD_DIM
    return {
        'x': jax.random.normal(ks[0], (BATCH, SEQ, D), f32),
        'attn_norm': gain(ks[1], (L, D)),
        'w_in': nrm(ks[2], (L, D, IN_WIDTH), D),
        'fox_forget_bias': 0.1 * jax.random.normal(ks[3], (L, MIX_HEADS), f32),
        'mla_q_norm': gain(ks[4], (L, MLA_Q_LORA)),
        'mla_kv_norm': gain(ks[5], (L, MLA_KV_LORA)),
        'mla_w_uq': nrm(ks[6], (L, MLA_Q_LORA, MIX_HEADS * (MLA_NOPE + MLA_ROPE)), MLA_Q_LORA),
        'mla_w_ukv': nrm(ks[7], (L, MLA_KV_LORA, MIX_HEADS * (MLA_NOPE + MLA_V)), MLA_KV_LORA),
        'nsa_cmp_k_pos': 0.02 * jax.random.normal(ks[8], (L, NSA_CMP_LEN, HEAD_DIM), f32),
        'nsa_cmp_k_w1': nrm(ks[9], (L, cmp_in, NSA_CMP_HIDDEN), cmp_in),
        'nsa_cmp_k_w2': nrm(ks[10], (L, NSA_CMP_HIDDEN, HEAD_DIM), NSA_CMP_HIDDEN),
        'nsa_cmp_v_pos': 0.02 * jax.random.normal(ks[11], (L, NSA_CMP_LEN, HEAD_DIM), f32),
        'nsa_cmp_v_w1': nrm(ks[12], (L, cmp_in, NSA_CMP_HIDDEN), cmp_in),
        'nsa_cmp_v_w2': nrm(ks[13], (L, NSA_CMP_HIDDEN, HEAD_DIM), NSA_CMP_HIDDEN),
        'w_branch': nrm(ks[14], (L, N_BRANCH, BRANCH_WIDTH, D), BRANCH_WIDTH),
        'w_out': nrm(ks[15], (L, D, D), D),
        'ffn_norm': gain(ks[16], (L, D)),
        'peer_w_q': nrm(ks[17], (L, D, PEER_HEADS * PEER_QDIM), D),
        'peer_sub_k1': nrm(ks[18], (L, PEER_HEADS, PEER_KEYS, PEER_QDIM // 2), PEER_QDIM // 2),
        'peer_sub_k2': nrm(ks[19], (L, PEER_HEADS, PEER_KEYS, PEER_QDIM // 2), PEER_QDIM // 2),
        'peer_u': nrm(ks[20], (L, PEER_EXPERTS, D), D),
        'peer_v': nrm(ks[21], (L, PEER_EXPERTS, D), PEER_HEADS),
        'final_norm': gain(ks[22], (D,)),
    }


def reference(x, attn_norm, w_in, fox_forget_bias, mla_q_norm, mla_kv_norm, mla_w_uq, mla_w_ukv,
              nsa_cmp_k_pos, nsa_cmp_k_w1, nsa_cmp_k_w2, nsa_cmp_v_pos, nsa_cmp_v_w1, nsa_cmp_v_w2,
              w_branch, w_out, ffn_norm, peer_w_q, peer_sub_k1, peer_sub_k2, peer_u, peer_v,
              final_norm):
    B, S, D = x.shape
    pos = jnp.arange(S)
    split_at = [int(c) for c in np.cumsum(IN_SPLITS)[:-1]]
    for l in range(DEPTH):
        h = rms_norm(x, attn_norm[l])
        (fq, fk, fv, ff, mcq, mckv, mkr, nq, nkc, nvc, nks, nvs, nkw, nvw, ng,
         sq, sk, sv, mg) = jnp.split(h @ w_in[l], split_at, axis=-1)

        log_f = jax.nn.log_sigmoid((ff + fox_forget_bias[l]).astype(jnp.float32))
        o_a = fox_attention(heads(fq, MIX_HEADS), heads(fk, MIX_HEADS), heads(fv, MIX_HEADS), log_f)
        o_b = mla_attention(mcq, mckv, mkr, mla_q_norm[l], mla_kv_norm[l], mla_w_uq[l], mla_w_ukv[l], pos)
        o_c = nsa_attention(heads(nq, MIX_HEADS), heads(nkc, NSA_KV_HEADS), heads(nvc, NSA_KV_HEADS),
                            heads(nks, NSA_KV_HEADS), heads(nvs, NSA_KV_HEADS),
                            heads(nkw, NSA_KV_HEADS), heads(nvw, NSA_KV_HEADS),
                            jax.nn.sigmoid(heads(ng, MIX_HEADS)), pos,
                            nsa_cmp_k_pos[l], nsa_cmp_k_w1[l], nsa_cmp_k_w2[l],
                            nsa_cmp_v_pos[l], nsa_cmp_v_w1[l], nsa_cmp_v_w2[l])
        o_d = stick_breaking_attention(heads(sq, MIX_HEADS), heads(sk, MIX_HEADS), heads(sv, MIX_HEADS))

        branches = jnp.stack([o.reshape(B, S, BRANCH_WIDTH) for o in (o_a, o_b, o_c, o_d)], axis=2)
        y = jnp.einsum('bsnc,ncd->bsnd', branches, w_branch[l])
        gate = jax.nn.sigmoid(mg.reshape(B, S, N_BRANCH, D))
        x = x + jnp.sum(gate * y, axis=2) @ w_out[l]

        x = x + peer_ffn(rms_norm(x, ffn_norm[l]), peer_w_q[l], peer_sub_k1[l], peer_sub_k2[l],
                         peer_u[l], peer_v[l])
    return rms_norm(x, final_norm)
```

```python
import functools
import math

import jax
import jax.numpy as jnp
from jax import lax
from jax.experimental import pallas as pl
from jax.experimental.pallas import tpu as pltpu

F32, BF16, I32 = jnp.float32, jnp.bfloat16, jnp.int32

D_MODEL = 2048
HEAD_DIM = 64
MIX_HEADS = 8
N_BRANCH = 4
BRANCH_WIDTH = MIX_HEADS * HEAD_DIM
ROPE_THETA = 500000.0
PARTIAL_ROT = HEAD_DIM // 4
RMS_EPS = 1e-6
NEG = -1e30

MLA_Q_LORA = D_MODEL // 4
MLA_KV_LORA = D_MODEL // 8
MLA_NOPE = HEAD_DIM
MLA_ROPE = HEAD_DIM // 2
MLA_V = HEAD_DIM

NSA_KV_HEADS = 2
NSA_REP = MIX_HEADS // NSA_KV_HEADS
NSA_KV_WIDTH = NSA_KV_HEADS * HEAD_DIM
NSA_CMP_LEN = 32
NSA_CMP_STRIDE = 16
NSA_CMP_HIDDEN = 256
NSA_SEL_LEN = 64
NSA_SEL_TOP = 16
NSA_WINDOW = 512
NSA_FORCE_BONUS = 1e3

PEER_HEADS = 8
PEER_KEYS = 128
PEER_EXPERTS = PEER_KEYS * PEER_KEYS
PEER_TOPK = 16
PEER_QDIM = 256
PEER_SLOTS = PEER_HEADS * PEER_TOPK

LANES = 128
SUBLANES = 8
VMEM_LIMIT = 56 * 1024 * 1024

_IN_NAMES = ("fq", "fk", "fv", "ff", "mcq", "mckv", "mkr", "nq", "nkc", "nvc", "nks", "nvs", "nkw", "nvw",
             "ng", "sq", "sk", "sv", "mg")
_IN_WIDTHS = (BRANCH_WIDTH, BRANCH_WIDTH, BRANCH_WIDTH, MIX_HEADS, MLA_Q_LORA, MLA_KV_LORA, MLA_ROPE,
              BRANCH_WIDTH, NSA_KV_WIDTH, NSA_KV_WIDTH, NSA_KV_WIDTH, NSA_KV_WIDTH, NSA_KV_WIDTH, NSA_KV_WIDTH,
              3 * MIX_HEADS, BRANCH_WIDTH, BRANCH_WIDTH, BRANCH_WIDTH, N_BRANCH * D_MODEL)
_IN_SRC = {}
_off = 0
for _n, _w in zip(_IN_NAMES, _IN_WIDTHS):
    _IN_SRC[_n] = (_off, _w)
    _off += _w

_SMALL_ORDER = ("fq", "fk", "fv", "mcq", "nq", "sq", "sk", "sv", "mckv", "ff", "mkr", "nkc", "nvc", "nks", "nvs",
                "nkw", "nvw", "ng")
_SEG = {}
_off = 0
for _n in _SMALL_ORDER:
    _w = -(-_IN_SRC[_n][1] // LANES) * LANES
    _SEG[_n] = (_off, _w)
    _off += _w
SMALL_WIDTH = -(-_off // 512) * 512


def _params(sem, vmem=None):
    return pltpu.CompilerParams(dimension_semantics=sem, vmem_limit_bytes=vmem)


def _mm(a, b):
    return jnp.dot(a, b, preferred_element_type=F32)


def _mm_nt(a, b):
    return lax.dot_general(a, b, (((1,), (1,)), ((), ())), preferred_element_type=F32)


def _split3(x):
    hi = x.astype(BF16)
    r = x - hi.astype(F32)
    mid = r.astype(BF16)
    lo = (r - mid.astype(F32)).astype(BF16)
    return hi, mid, lo


def _gelu(x):
    return 0.5 * x * (1.0 + lax.erf(x * (1.0 / math.sqrt(2.0))))


def _log_sigmoid(x):
    return jnp.minimum(x, 0.0) - jnp.log1p(jnp.exp(-jnp.abs(x)))


def _sigmoid(x):
    return 1.0 / (1.0 + jnp.exp(-x))


def _rmsnorm_kernel(x_ref, g_ref, o_ref):
    x = x_ref[...]
    y = x * lax.rsqrt(jnp.mean(x * x, axis=-1, keepdims=True) + RMS_EPS)
    o_ref[...] = (y * g_ref[...]).astype(o_ref.dtype)


def rmsnorm(x, g, out_dtype, tm=256):
    T, D = x.shape
    return pl.pallas_call(
        _rmsnorm_kernel,
        grid=(T // tm,),
        in_specs=[pl.BlockSpec((tm, D), lambda i: (i, 0)), pl.BlockSpec((1, D), lambda i: (0, 0))],
        out_specs=pl.BlockSpec((tm, D), lambda i: (i, 0)),
        out_shape=jax.ShapeDtypeStruct((T, D), out_dtype),
        compiler_params=_params(("parallel",)),
        name="rmsnorm",
    )(x, g.reshape(1, D))


def _matmul_kernel(a_ref, b_ref, *rest):
    o_ref = rest[-1]
    acc = _mm(a_ref[...], b_ref[...])
    if len(rest) == 2:
        acc = acc + rest[0][...]
    o_ref[...] = acc.astype(o_ref.dtype)


def matmul(a, b, res=None, out_dtype=F32, tm=1024, tn=512):
    M, K = a.shape
    N = b.shape[1]
    tm, tn = min(tm, M), min(tn, N)
    in_specs = [pl.BlockSpec((tm, K), lambda j, i: (i, 0)), pl.BlockSpec((K, tn), lambda j, i: (0, j))]
    args = [a, b]
    if res is not None:
        in_specs.append(pl.BlockSpec((tm, tn), lambda j, i: (i, j)))
        args.append(res)
    return pl.pallas_call(
        _matmul_kernel,
        grid=(N // tn, M // tm),
        in_specs=in_specs,
        out_specs=pl.BlockSpec((tm, tn), lambda j, i: (i, j)),
        out_shape=jax.ShapeDtypeStruct((M, N), out_dtype),
        compiler_params=_params(("parallel", "parallel"), VMEM_LIMIT),
        name="matmul",
    )(*args)


def _rope_roll(x, cos, sin, sh, period):
    w = x.shape[-1]
    lane = lax.broadcasted_iota(I32, x.shape, 1) & (period - 1)
    sa = jnp.where(lane < sh, -sin, 0.0)
    sb = jnp.where((lane >= sh) & (lane < 2 * sh), sin, 0.0)
    x_up = pltpu.roll(x, w - sh, 1)
    x_dn = pltpu.roll(x, sh, 1)
    return x * cos + x_up * sa + x_dn * sb


def _prep_kernel(ps_ref, fb_ref, gq_ref, gkv_ref, wuq_ref, wukv_ref, ncos_ref, nsin_ref, mcos_ref, msin_ref,
                 fq_o, fk_o, fv_o, lf_o, qn_o, qr_o, kn_o, mv_o, kr_o,
                 nq_o, nkc_o, nvc_o, nks_o, nvs_o, nkw_o, nvw_o, ng_o, sq_o, sk_o, sv_o):
    def seg(name):
        off, w = _SEG[name]
        return ps_ref[:, off:off + w]

    qs = HEAD_DIM ** -0.5
    fq_o[...] = (seg("fq") * qs).astype(BF16)
    fk_o[...] = seg("fk").astype(BF16)
    fv_o[...] = seg("fv").astype(BF16)
    lf_o[...] = _log_sigmoid(seg("ff") + fb_ref[...])
    sq_o[...] = (seg("sq") * qs).astype(BF16)
    sk_o[...] = seg("sk").astype(BF16)
    sv_o[...] = seg("sv").astype(BF16)
    cq = seg("mcq")
    hq = cq * lax.rsqrt(jnp.mean(cq * cq, axis=-1, keepdims=True) + RMS_EPS) * gq_ref[...]
    qa = _mm(hq.astype(BF16), wuq_ref[...])
    mcos, msin = mcos_ref[...], msin_ref[...]
    x1, x2 = qa[:, BRANCH_WIDTH:BRANCH_WIDTH + LANES], qa[:, BRANCH_WIDTH + LANES:]
    qn_o[...] = qa[:, :BRANCH_WIDTH].astype(BF16)
    qr_o[:, :LANES] = (x1 * mcos - x2 * msin).astype(BF16)
    qr_o[:, LANES:] = (x1 * msin + x2 * mcos).astype(BF16)
    ckv = seg("mckv")
    hkv = ckv * lax.rsqrt(jnp.mean(ckv * ckv, axis=-1, keepdims=True) + RMS_EPS) * gkv_ref[...]
    kva = _mm(hkv.astype(BF16), wukv_ref[...])
    kn_o[...] = kva[:, :BRANCH_WIDTH].astype(BF16)
    mv_o[...] = kva[:, BRANCH_WIDTH:].astype(BF16)
    kr_o[...] = _rope_roll(seg("mkr"), mcos, msin, MLA_ROPE // 2, LANES).astype(BF16)
    ncos, nsin = ncos_ref[...], nsin_ref[...]
    kcos, ksin = ncos[:, :NSA_KV_WIDTH], nsin[:, :NSA_KV_WIDTH]
    nq_o[...] = (_rope_roll(seg("nq"), ncos, nsin, PARTIAL_ROT // 2, HEAD_DIM) * qs).astype(BF16)
    nks_o[...] = _rope_roll(seg("nks"), kcos, ksin, PARTIAL_ROT // 2, HEAD_DIM).astype(BF16)
    nkw_o[...] = _rope_roll(seg("nkw"), kcos, ksin, PARTIAL_ROT // 2, HEAD_DIM).astype(BF16)
    nkc_o[...] = seg("nkc").astype(BF16)
    nvc_o[...] = seg("nvc").astype(BF16)
    nvs_o[...] = seg("nvs").astype(BF16)
    nvw_o[...] = seg("nvw").astype(BF16)
    ng_o[...] = seg("ng")


def prep(ps, fb, gq, gkv, wuq, wukv, ncos, nsin, mcos, msin, S, tm=256):
    T = ps.shape[0]
    n_pos = S // tm

    def tok(w):
        return pl.BlockSpec((tm, w), lambda i: (i, 0))

    def const(shape):
        return pl.BlockSpec(shape, lambda i: (0, 0))

    def pos(w):
        return pl.BlockSpec((tm, w), lambda i: (i % n_pos, 0))

    outs = [("fq", 512, BF16), ("fk", 512, BF16), ("fv", 512, BF16), ("lf", 128, F32),
            ("qn", 512, BF16), ("qr", 256, BF16), ("kn", 512, BF16), ("mv", 512, BF16), ("kr", 128, BF16),
            ("nq", 512, BF16), ("nkc", 128, BF16), ("nvc", 128, BF16), ("nks", 128, BF16), ("nvs", 128, BF16),
            ("nkw", 128, BF16), ("nvw", 128, BF16), ("ng", 128, F32),
            ("sq", 512, BF16), ("sk", 512, BF16), ("sv", 512, BF16)]
    res = pl.pallas_call(
        _prep_kernel,
        grid=(T // tm,),
        in_specs=[tok(SMALL_WIDTH), const(fb.shape), const(gq.shape), const(gkv.shape), const(wuq.shape),
                  const(wukv.shape), pos(512), pos(512), pos(128), pos(128)],
        out_specs=[tok(w) for _, w, _ in outs],
        out_shape=[jax.ShapeDtypeStruct((T, w), dt) for _, w, dt in outs],
        compiler_params=_params(("parallel",), VMEM_LIMIT),
        name="prep",
    )(ps, fb, gq, gkv, wuq, wukv, ncos, nsin, mcos, msin)
    return {n: r for (n, _, _), r in zip(outs, res)}


def _cumsum_kernel(x_ref, o_ref, *, S, ch):
    r = lax.broadcasted_iota(I32, (ch, ch), 0)
    c = lax.broadcasted_iota(I32, (ch, ch), 1)
    tri = jnp.where(c <= r, 1.0, 0.0).astype(BF16)
    carry = jnp.zeros((1, x_ref.shape[-1]), F32)
    for i in range(S // ch):
        hi, mid, lo = _split3(x_ref[i * ch:(i + 1) * ch, :])
        y = _mm(tri, hi) + _mm(tri, mid) + _mm(tri, lo) + carry
        o_ref[i * ch:(i + 1) * ch, :] = y
        carry = y[ch - 1:ch, :]


def cumsum_tokens(x, S, ch=256):
    T, W = x.shape
    return pl.pallas_call(
        functools.partial(_cumsum_kernel, S=S, ch=ch),
        grid=(T // S,),
        in_specs=[pl.BlockSpec((S, W), lambda b: (b, 0))],
        out_specs=pl.BlockSpec((S, W), lambda b: (b, 0)),
        out_shape=jax.ShapeDtypeStruct((T, W), F32),
        compiler_params=_params(("parallel",)),
        name="fox_cumsum",
    )(x)


def _attn_kernel(*refs, R, tq, tk, mode, window, scale, rope2, fox, gated):
    refs = list(refs)
    q_ref, k_ref, v_ref = refs[:3]
    pos = 3
    if rope2:
        q2_ref, k2_ref = refs[pos:pos + 2]
        pos += 2
    if fox:
        fq_ref, fk_ref = refs[pos:pos + 2]
        pos += 2
    if mode == "select":
        sel_ref, emat_ref = refs[pos:pos + 2]
        pos += 2
    if gated:
        gate_ref = refs[pos]
        pos += 1
    o_ref = refs[pos]

    rq = R * tq
    dk, dv = q_ref.shape[-1], v_ref.shape[-1]
    q0 = pl.program_id(1) * tq
    q = q_ref[0].reshape(rq, dk)
    qpos1 = q0 + lax.broadcasted_iota(I32, (tq, 1), 0)
    qpos = jnp.concatenate([qpos1] * R, axis=0) if R > 1 else qpos1
    if rope2:
        q2 = q2_ref[0].reshape(rq, q2_ref.shape[-1])
    if fox:
        fq = fq_ref[0].reshape(rq, 1)
    if mode == "select":
        sel = sel_ref[0]

    def body(j, carry):
        m, l, acc = carry
        k0 = pl.multiple_of(j * tk, tk)
        s = _mm_nt(q, k_ref[0, pl.ds(k0, tk), :])
        if rope2:
            s = s + _mm_nt(q2, k2_ref[0, pl.ds(k0, tk), :])
        if scale is not None:
            s = s * scale
        if fox:
            s = s + fq - fk_ref[0, :, pl.ds(k0, tk)]
        kpos = k0 + lax.broadcasted_iota(I32, (1, tk), 1)
        mask = kpos <= qpos
        if mode == "window":
            mask = mask & (kpos > qpos - window)
        if mode == "select":
            picked = _mm(sel, emat_ref[:, pl.ds(k0, tk)])
            mask = mask & ((jnp.concatenate([picked] * R, axis=0) if R > 1 else picked) > 0.5)
        s = jnp.where(mask, s, NEG)
        m_new = jnp.maximum(m, jnp.max(s, axis=-1, keepdims=True))
        alpha = jnp.exp(m - m_new)
        p = jnp.where(mask, jnp.exp(s - m_new), 0.0)
        l = alpha * l + jnp.sum(p, axis=-1, keepdims=True)
        acc = alpha * acc + _mm(p.astype(BF16), v_ref[0, pl.ds(k0, tk), :])
        return m_new, l, acc

    lo = jnp.maximum(q0 - (window - 1), 0) // tk if mode == "window" else 0
    hi = (q0 + tq - 1) // tk + 1
    init = (jnp.full((rq, 1), NEG, F32), jnp.zeros((rq, 1), F32), jnp.zeros((rq, dv), F32))
    _, l, acc = lax.fori_loop(lo, hi, body, init)
    o = acc / l
    if gated:
        o = o * _sigmoid(gate_ref[0].reshape(rq, 1))
    o_ref[0] = o.reshape(R, tq, dv).astype(o_ref.dtype)


def attention(q, k, v, *, mode="causal", tq, tk, window=None, scale=None, q2=None, k2=None, k2_group=1,
              fq=None, fk=None, sel=None, emat=None, gate=None):
    N, R, S, dk = q.shape
    dv = v.shape[-1]

    def qspec(w):
        return pl.BlockSpec((1, R, tq, w), lambda n, i: (n, 0, i, 0))

    def kvspec(w):
        return pl.BlockSpec((1, S, w), lambda n, i: (n, 0, 0))

    in_specs = [qspec(dk), kvspec(dk), kvspec(dv)]
    args = [q, k, v]
    if q2 is not None:
        in_specs += [qspec(q2.shape[-1]), pl.BlockSpec((1, S, k2.shape[-1]), lambda n, i: (n // k2_group, 0, 0))]
        args += [q2, k2]
    if fq is not None:
        in_specs += [qspec(1), pl.BlockSpec((1, 1, S), lambda n, i: (n, 0, 0))]
        args += [fq, fk]
    if mode == "select":
        in_specs += [pl.BlockSpec((1, tq, sel.shape[-1]), lambda n, i: (n, i, 0)),
                     pl.BlockSpec(emat.shape, lambda n, i: (0, 0))]
        args += [sel, emat]
    if gate is not None:
        in_specs.append(qspec(1))
        args.append(gate)
    kern = functools.partial(_attn_kernel, R=R, tq=tq, tk=tk, mode=mode, window=window, scale=scale,
                             rope2=q2 is not None, fox=fq is not None, gated=gate is not None)
    return pl.pallas_call(
        kern,
        grid=(N, S // tq),
        in_specs=in_specs,
        out_specs=qspec(dv),
        out_shape=jax.ShapeDtypeStruct((N, R, S, dv), F32),
        compiler_params=_params(("parallel", "parallel"), VMEM_LIMIT),
        name="attn_" + mode,
    )(*args)


def _stickbreak_kernel(q_ref, k_ref, v_ref, o_ref, *, tq, tk):
    q0 = pl.program_id(1) * tq
    q = q_ref[0]
    qpos = q0 + lax.broadcasted_iota(I32, (tq, 1), 0)
    r = lax.broadcasted_iota(I32, (tk, tk), 0)
    c = lax.broadcasted_iota(I32, (tk, tk), 1)
    later = jnp.where(r > c, 1.0, 0.0).astype(BF16)
    n_kv = (q0 + tq - 1) // tk + 1

    def body(jj, carry):
        tail, acc = carry
        k0 = pl.multiple_of((n_kv - 1 - jj) * tk, tk)
        z = _mm_nt(q, k_ref[0, pl.ds(k0, tk), :])
        soft = jnp.log1p(jnp.exp(-jnp.abs(z)))
        mask = (k0 + lax.broadcasted_iota(I32, (1, tk), 1)) < qpos
        log_beta = jnp.minimum(z, 0.0) - soft
        log_1m = jnp.where(mask, -jnp.maximum(z, 0.0) - soft, 0.0)
        hi, mid, lo = _split3(log_1m)
        inner = _mm(hi, later) + _mm(mid, later) + _mm(lo, later)
        a = jnp.where(mask, jnp.exp(log_beta + (tail + inner)), 0.0)
        acc = acc + _mm(a.astype(BF16), v_ref[0, pl.ds(k0, tk), :])
        return tail + jnp.sum(log_1m, axis=-1, keepdims=True), acc

    init = (jnp.zeros((tq, 1), F32), jnp.zeros((tq, v_ref.shape[-1]), F32))
    _, acc = lax.fori_loop(0, n_kv, body, init)
    o_ref[0] = acc


def stickbreak_attention(q, k, v, tq=128, tk=128):
    N, S, dk = q.shape
    return pl.pallas_call(
        functools.partial(_stickbreak_kernel, tq=tq, tk=tk),
        grid=(N, S // tq),
        in_specs=[pl.BlockSpec((1, tq, dk), lambda n, i: (n, i, 0)),
                  pl.BlockSpec((1, S, dk), lambda n, i: (n, 0, 0)),
                  pl.BlockSpec((1, S, dk), lambda n, i: (n, 0, 0))],
        out_specs=pl.BlockSpec((1, tq, dk), lambda n, i: (n, i, 0)),
        out_shape=jax.ShapeDtypeStruct((N, S, dk), F32),
        compiler_params=_params(("parallel", "parallel"), VMEM_LIMIT),
        name="stickbreak",
    )(q, k, v)


def _compress_kernel(u_ref, pos_ref, w1_ref, w2_ref, w2s_ref, cos_ref, sin_ref, o_ref, *, rotary):
    u = u_ref[0]
    half = u.shape[-1]
    n = u.shape[0]
    first = _mm(u, w1_ref[:half, :])
    second = _mm(u, w1_ref[half:, :])
    bias = _mm(pos_ref[...], w1_ref[...])[0:1, :]
    hid = _gelu(first + pltpu.roll(second, n - 1, 0) + bias).astype(BF16)
    out = _mm(hid, w2_ref[...])
    if rotary:
        out = out * cos_ref[...] + _mm(hid, w2s_ref[...]) * sin_ref[...]
    o_ref[0] = out.astype(o_ref.dtype)


def compress(u, pos8, w1, w2, w2s, cos, sin, rotary):
    N, n, half = u.shape

    def const(a):
        return pl.BlockSpec(a.shape, lambda i: (0,) * a.ndim)

    return pl.pallas_call(
        functools.partial(_compress_kernel, rotary=rotary),
        grid=(N,),
        in_specs=[pl.BlockSpec((1, n, half), lambda i: (i, 0, 0)), const(pos8), const(w1), const(w2), const(w2s),
                  const(cos), const(sin)],
        out_specs=pl.BlockSpec((1, n, HEAD_DIM), lambda i: (i, 0, 0)),
        out_shape=jax.ShapeDtypeStruct((N, n, HEAD_DIM), BF16),
        compiler_params=_params(("parallel",)),
        name="nsa_compress",
    )(u, pos8, w1, w2, w2s, cos, sin)


def _cmp_attn_kernel(q_ref, kc_ref, vc_ref, gate_ref, c2s_ref, o_ref, sel_ref, *, R, tq, n_cmp, n_top):
    rq = R * tq
    nc = kc_ref.shape[1]
    n_sel = c2s_ref.shape[-1]
    q0 = pl.program_id(1) * tq
    q = q_ref[0].reshape(rq, q_ref.shape[-1])
    qpos1 = q0 + lax.broadcasted_iota(I32, (tq, 1), 0)
    qpos = jnp.concatenate([qpos1] * R, axis=0)
    cidx = lax.broadcasted_iota(I32, (1, nc), 1)
    cmask = (cidx * NSA_CMP_STRIDE + (NSA_CMP_LEN - 1) <= qpos) & (cidx < n_cmp)
    s = jnp.where(cmask, _mm_nt(q, kc_ref[0]), NEG)
    e = jnp.where(cmask, jnp.exp(s - jnp.max(s, axis=-1, keepdims=True)), 0.0)
    den = jnp.sum(e, axis=-1, keepdims=True)
    p = e / jnp.where(den > 0.0, den, 1.0)
    o = _mm(p.astype(BF16), vc_ref[0]) * _sigmoid(gate_ref[0].reshape(rq, 1))
    o_ref[0] = o.reshape(R, tq, o.shape[-1])

    psum = p[0:tq]
    for r in range(1, R):
        psum = psum + p[r * tq:(r + 1) * tq]
    hi, mid, lo = _split3(psum)
    c2s = c2s_ref[...]
    imp = _mm(hi, c2s) + _mm(mid, c2s) + _mm(lo, c2s)
    blk = lax.broadcasted_iota(I32, (1, n_sel), 1)
    cur = lax.shift_right_logical(qpos1, int(math.log2(NSA_SEL_LEN)))
    forced = ((blk == 0) | (blk == cur) | (blk == cur - 1)).astype(F32)
    valid = blk * NSA_SEL_LEN <= qpos1
    score = jnp.where(valid, imp + NSA_FORCE_BONUS * forced, NEG)
    rank = jnp.zeros((tq, n_sel), F32)
    for i in range(n_sel):
        si = score[:, i:i + 1]
        ahead = (si > score) | ((si == score) & (blk > i))
        rank = rank + jnp.where(ahead, 1.0, 0.0)
    sel_ref[0] = jnp.where((rank < n_top) & valid, 1.0, 0.0).astype(sel_ref.dtype)


def cmp_attention(q, kc, vc, gate, c2s, n_cmp, n_top, tq=128):
    N, R, S, dk = q.shape
    n_sel = c2s.shape[-1]

    def qspec(w):
        return pl.BlockSpec((1, R, tq, w), lambda n, i: (n, 0, i, 0))

    return pl.pallas_call(
        functools.partial(_cmp_attn_kernel, R=R, tq=tq, n_cmp=n_cmp, n_top=n_top),
        grid=(N, S // tq),
        in_specs=[qspec(dk), pl.BlockSpec((1,) + kc.shape[1:], lambda n, i: (n, 0, 0)),
                  pl.BlockSpec((1,) + vc.shape[1:], lambda n, i: (n, 0, 0)), qspec(1),
                  pl.BlockSpec(c2s.shape, lambda n, i: (0, 0))],
        out_specs=[qspec(dk), pl.BlockSpec((1, tq, n_sel), lambda n, i: (n, i, 0))],
        out_shape=[jax.ShapeDtypeStruct((N, R, S, dk), F32), jax.ShapeDtypeStruct((N, S, n_sel), BF16)],
        compiler_params=_params(("parallel", "parallel")),
        name="nsa_cmp_attn",
    )(q, kc, vc, gate, c2s)


def _merge_kernel(oa_ref, ob_ref, oc1_ref, oc2_ref, oc3_ref, od_ref, g0_ref, g1_ref, g2_ref, g3_ref, wb_ref, z_ref):
    branches = (oa_ref[...], ob_ref[...], oc1_ref[...] + oc2_ref[...] + oc3_ref[...], od_ref[...])
    gates = (g0_ref, g1_ref, g2_ref, g3_ref)
    z = None
    for n in range(N_BRANCH):
        y = _sigmoid(gates[n][...]) * _mm(branches[n].astype(BF16), wb_ref[n])
        z = y if z is None else z + y
    z_ref[...] = z.astype(z_ref.dtype)


def merge(outs, mg, wb, tm=512, tn=512):
    T = mg.shape[0]
    D = wb.shape[-1]
    nj = D // tn
    ospec = pl.BlockSpec((tm, BRANCH_WIDTH), lambda i, j: (i, 0))
    gspecs = [pl.BlockSpec((tm, tn), functools.partial(lambda i, j, n: (i, n * nj + j), n=n)) for n in range(N_BRANCH)]
    return pl.pallas_call(
        _merge_kernel,
        grid=(T // tm, nj),
        in_specs=[ospec] * 6 + gspecs + [pl.BlockSpec((N_BRANCH, BRANCH_WIDTH, tn), lambda i, j: (0, 0, j))],
        out_specs=pl.BlockSpec((tm, tn), lambda i, j: (i, j)),
        out_shape=jax.ShapeDtypeStruct((T, D), BF16),
        compiler_params=_params(("parallel", "parallel"), VMEM_LIMIT),
        name="merge",
    )(*outs, mg, mg, mg, mg, wb)


def _topk_rows(s, k, payload=None):
    n = s.shape[0]
    row = lax.broadcasted_iota(I32, s.shape, 0).astype(F32)
    vals, rows, pays = [], [], []
    for _ in range(k):
        m = jnp.max(s, axis=0, keepdims=True)
        am = jnp.min(jnp.where(s == m, row, float(n)), axis=0, keepdims=True)
        hit = row == am
        vals.append(m)
        rows.append(am)
        if payload is not None:
            pays.append(jnp.sum(jnp.where(hit, payload, 0.0), axis=0, keepdims=True))
        s = jnp.where(hit, -jnp.inf, s)
    vals, rows = jnp.concatenate(vals, axis=0), jnp.concatenate(rows, axis=0)
    return (vals, rows) if payload is None else (vals, rows, jnp.concatenate(pays, axis=0))


def _peer_topk_kernel(q_ref, k1_ref, k2_ref, i1_ref, i2_ref, g_ref):
    q = q_ref[...].astype(BF16)
    half = PEER_QDIM // 2
    v1, a1 = _topk_rows(_mm_nt(k1_ref[0], q[:, :half]), PEER_TOPK)
    v2, a2 = _topk_rows(_mm_nt(k2_ref[0], q[:, half:]), PEER_TOPK)
    cand = jnp.concatenate([v1[a:a + 1] + v2 for a in range(PEER_TOPK)], axis=0)
    cidx = jnp.concatenate([a1[a:a + 1] * float(PEER_KEYS) + a2 for a in range(PEER_TOPK)], axis=0)
    top_s, _, eidx = _topk_rows(cand, PEER_TOPK, payload=cidx)
    ex = jnp.exp(top_s - top_s[0:1])
    g_ref[0] = ex / jnp.sum(ex, axis=0, keepdims=True)
    e = eidx.astype(I32)
    i1_ref[0] = lax.shift_right_logical(e, int(math.log2(PEER_KEYS)))
    i2_ref[0] = e & (PEER_KEYS - 1)


def peer_topk(q, k1, k2, tm=256):
    T = q.shape[0]
    H = k1.shape[0]
    ospec = pl.BlockSpec((1, PEER_TOPK, tm), lambda i, h: (h, 0, i))
    kspec = pl.BlockSpec((1,) + k1.shape[1:], lambda i, h: (h, 0, 0))
    return pl.pallas_call(
        _peer_topk_kernel,
        grid=(T // tm, H),
        in_specs=[pl.BlockSpec((tm, PEER_QDIM), lambda i, h: (i, h)), kspec, kspec],
        out_specs=[ospec, ospec, ospec],
        out_shape=[jax.ShapeDtypeStruct((H, PEER_TOPK, T), I32), jax.ShapeDtypeStruct((H, PEER_TOPK, T), I32),
                   jax.ShapeDtypeStruct((H, PEER_TOPK, T), F32)],
        compiler_params=_params(("parallel", "parallel")),
        name="peer_topk",
    )(q, k1, k2)


def _peer_gate_kernel(i1_ref, i2_ref, g_ref, o_ref, *, tm, nr):
    i1, i2, g = i1_ref[...], i2_ref[...], g_ref[...]
    n = PEER_KEYS
    g_hi = g.astype(BF16).astype(F32)
    g_lo = g - g_hi
    row = lax.broadcasted_iota(I32, (tm, n, PEER_SLOTS), 1)
    first = i1 == row
    a = jnp.concatenate([jnp.where(first, g_hi, 0.0).astype(BF16), jnp.where(first, g_lo, 0.0).astype(BF16)], axis=-1)
    second = jnp.where(i2 == row, 1.0, 0.0).astype(BF16)
    b = jnp.concatenate([second, second], axis=-1)
    w = lax.dot_general(a, b, (((2,), (2,)), ((0,), (0,))), preferred_element_type=F32)
    for ch in range(n // nr):
        o_ref[ch] = w[:, ch * nr:(ch + 1) * nr, :].reshape(tm * nr, n)


def peer_gates(i1, i2, g, nr=SUBLANES, tm=32):
    T = i1.shape[0]
    n = PEER_KEYS
    spec = pl.BlockSpec((tm, 1, PEER_SLOTS), lambda i: (i, 0, 0))
    return pl.pallas_call(
        functools.partial(_peer_gate_kernel, tm=tm, nr=nr),
        grid=(T // tm,),
        in_specs=[spec, spec, spec],
        out_specs=pl.BlockSpec((n // nr, tm * nr, n), lambda i: (0, i, 0)),
        out_shape=jax.ShapeDtypeStruct((n // nr, T * nr, n), F32),
        compiler_params=_params(("parallel",), VMEM_LIMIT),
        name="peer_gates",
    )(i1, i2, g)


def _peer_dense_kernel(h_ref, ut_ref, v_ref, w_ref, res_ref, o_ref, *, tm, nr):
    @pl.when(pl.program_id(1) == 0)
    def _():
        o_ref[...] = res_ref[...]

    act = _gelu(_mm(h_ref[...], ut_ref[...]))
    n = PEER_KEYS
    cols = []
    for r in range(nr):
        w_r = w_ref[pl.ds(r, tm, stride=nr), :]
        cols.append((w_r * act[:, r * n:(r + 1) * n]).astype(BF16))
    o_ref[...] += _mm(jnp.concatenate(cols, axis=1), v_ref[...])


def peer_dense(h, ut, v, w, res, nr=SUBLANES, tm=512):
    T, D = h.shape
    E = v.shape[0]
    te = nr * PEER_KEYS
    return pl.pallas_call(
        functools.partial(_peer_dense_kernel, tm=tm, nr=nr),
        grid=(T // tm, E // te),
        in_specs=[pl.BlockSpec((tm, D), lambda i, e: (i, 0)),
                  pl.BlockSpec((D, te), lambda i, e: (0, e)),
                  pl.BlockSpec((te, D), lambda i, e: (e, 0)),
                  pl.BlockSpec((None, tm * nr, PEER_KEYS), lambda i, e: (e, i, 0)),
                  pl.BlockSpec((tm, D), lambda i, e: (i, 0))],
        out_specs=pl.BlockSpec((tm, D), lambda i, e: (i, 0)),
        out_shape=jax.ShapeDtypeStruct((T, D), F32),
        compiler_params=_params(("parallel", "arbitrary"), VMEM_LIMIT),
        name="peer_dense",
    )(h, ut, v, w, res)


def _to_heads(x, B, S, H):
    return x.reshape(B, S, H, -1).transpose(0, 2, 1, 3).reshape(B * H, S, -1)


def _from_heads(x, B, S, H):
    return x.reshape(B, H, S, -1).transpose(0, 2, 1, 3).reshape(B * S, -1)


def _to_groups(x, B, S, G, R):
    return x.reshape(B, S, G, R, -1).transpose(0, 2, 3, 1, 4).reshape(B * G, R, S, -1)


def _from_groups(x, B, S, G, R):
    return x.reshape(B, G, R, S, -1).transpose(0, 3, 1, 2, 4).reshape(B * S, -1)


def _rope_tables(pos, half, reps, width):
    inv = jnp.exp(-math.log(ROPE_THETA) * jnp.arange(half, dtype=F32) / half)
    ang = pos.astype(F32)[:, None] * inv[None, :]
    pad = width - 2 * half
    cos = jnp.concatenate([jnp.cos(ang), jnp.cos(ang), jnp.ones((pos.shape[0], pad), F32)], axis=-1)
    sin = jnp.concatenate([jnp.sin(ang), jnp.sin(ang), jnp.zeros((pos.shape[0], pad), F32)], axis=-1)
    return jnp.tile(cos, (1, reps)), jnp.tile(sin, (1, reps))


def _small_weight(w_in_l):
    cols = []
    for name in _SMALL_ORDER:
        off, w = _IN_SRC[name]
        cols.append(w_in_l[:, off:off + w])
        pad = _SEG[name][1] - w
        if pad:
            cols.append(jnp.zeros((w_in_l.shape[0], pad), w_in_l.dtype))
    used = _SEG[_SMALL_ORDER[-1]][0] + _SEG[_SMALL_ORDER[-1]][1]
    if SMALL_WIDTH > used:
        cols.append(jnp.zeros((w_in_l.shape[0], SMALL_WIDTH - used), w_in_l.dtype))
    return jnp.concatenate(cols, axis=1).astype(BF16)


def _layer(x, B, S, p):
    T = B * S
    H, G, R = MIX_HEADS, NSA_KV_HEADS, NSA_REP
    pos = jnp.arange(S)

    h = rmsnorm(x, p["attn_norm"], BF16)
    ps = matmul(h, _small_weight(p["w_in"]))
    mg_off, mg_w = _IN_SRC["mg"]
    mg = matmul(h, p["w_in"][:, mg_off:mg_off + mg_w].astype(BF16))

    wuq = p["mla_w_uq"].reshape(MLA_Q_LORA, H, MLA_NOPE + MLA_ROPE)
    r_half = MLA_ROPE // 2
    wuq = jnp.concatenate([wuq[:, :, :MLA_NOPE].reshape(MLA_Q_LORA, -1),
                           wuq[:, :, MLA_NOPE:MLA_NOPE + r_half].reshape(MLA_Q_LORA, -1),
                           wuq[:, :, MLA_NOPE + r_half:].reshape(MLA_Q_LORA, -1)], axis=1).astype(BF16)
    wukv = p["mla_w_ukv"].reshape(MLA_KV_LORA, H, MLA_NOPE + MLA_V)
    wukv = jnp.concatenate([wukv[:, :, :MLA_NOPE].reshape(MLA_KV_LORA, -1),
                            wukv[:, :, MLA_NOPE:].reshape(MLA_KV_LORA, -1)], axis=1).astype(BF16)
    ncos, nsin = _rope_tables(pos, PARTIAL_ROT // 2, H, HEAD_DIM)
    mcos, msin = _rope_tables(pos, r_half, LANES // MLA_ROPE, MLA_ROPE)
    fb = jnp.pad(p["fox_forget_bias"], (0, LANES - H)).reshape(1, LANES)
    t = prep(ps, fb, p["mla_q_norm"].reshape(1, -1), p["mla_kv_norm"].reshape(1, -1), wuq, wukv,
             ncos, nsin, mcos, msin, S)

    fcum = cumsum_tokens(t["lf"], S)[:, :H]
    fcum = fcum.reshape(B, S, H).transpose(0, 2, 1).reshape(B * H, S)
    o_a = attention(_to_heads(t["fq"], B, S, H)[:, None], _to_heads(t["fk"], B, S, H), _to_heads(t["fv"], B, S, H),
                    tq=256, tk=128, fq=fcum[:, None, :, None], fk=fcum[:, None, :])
    o_a = _from_heads(o_a[:, 0], B, S, H)

    qr = t["qr"].reshape(B, S, 2, H, r_half).transpose(0, 3, 1, 2, 4).reshape(B * H, S, MLA_ROPE)
    kr = t["kr"][:, :MLA_ROPE].reshape(B, S, MLA_ROPE)
    o_b = attention(_to_heads(t["qn"], B, S, H)[:, None], _to_heads(t["kn"], B, S, H), _to_heads(t["mv"], B, S, H),
                    tq=256, tk=128, scale=(MLA_NOPE + MLA_ROPE) ** -0.5, q2=qr[:, None], k2=kr, k2_group=H)
    o_b = _from_heads(o_b[:, 0], B, S, H)

    n_half = S // NSA_CMP_STRIDE
    n_cmp = (S - NSA_CMP_LEN) // NSA_CMP_STRIDE + 1
    n_sel = S // NSA_SEL_LEN
    cmp_end = jnp.arange(n_half) * NSA_CMP_STRIDE + NSA_CMP_LEN - 1
    ccos, csin = _rope_tables(cmp_end, PARTIAL_ROT // 2, 1, HEAD_DIM)
    d = jnp.arange(HEAD_DIM)
    csin = jnp.where(d < PARTIAL_ROT // 2, -csin, csin)
    swap = jnp.where(d < PARTIAL_ROT // 2, d + PARTIAL_ROT // 2, d - PARTIAL_ROT // 2)

    def compressed(tok, pos_emb, w1, w2, rotary):
        u = _to_heads(tok, B, S, G).reshape(B * G, n_half, NSA_CMP_STRIDE * HEAD_DIM)
        pos8 = jnp.pad(pos_emb.reshape(1, -1), ((0, SUBLANES - 1), (0, 0))).astype(BF16)
        w2s = jnp.where(d[None, :] < PARTIAL_ROT, w2[:, swap], 0.0)
        return compress(u, pos8, w1.astype(BF16), w2.astype(BF16), w2s.astype(BF16), ccos, csin, rotary)

    kc = compressed(t["nkc"], p["nsa_cmp_k_pos"], p["nsa_cmp_k_w1"], p["nsa_cmp_k_w2"], True)
    vc = compressed(t["nvc"], p["nsa_cmp_v_pos"], p["nsa_cmp_v_w1"], p["nsa_cmp_v_w2"], False)
    cmp_start = jnp.arange(n_half) * NSA_CMP_STRIDE
    sel_lo = jnp.arange(n_sel) * NSA_SEL_LEN
    cover = jnp.clip(jnp.minimum(cmp_start[:, None] + NSA_CMP_LEN, sel_lo[None, :] + NSA_SEL_LEN)
                     - jnp.maximum(cmp_start[:, None], sel_lo[None, :]), 0, None)
    c2s = (cover.astype(F32) / NSA_CMP_LEN).astype(BF16)
    emat = (jnp.arange(S)[None, :] // NSA_SEL_LEN == jnp.arange(n_sel)[:, None]).astype(BF16)
    ng = t["ng"][:, :3 * H].reshape(B, S, G, R, 3).transpose(4, 0, 2, 3, 1).reshape(3, B * G, R, S, 1)
    nq = _to_groups(t["nq"], B, S, G, R)
    o_cmp, sel = cmp_attention(nq, kc, vc, ng[0], c2s, n_cmp, min(NSA_SEL_TOP, n_sel))
    o_slc = attention(nq, _to_heads(t["nks"], B, S, G), _to_heads(t["nvs"], B, S, G), mode="select", tq=128, tk=128,
                      sel=sel, emat=emat, gate=ng[1])
    o_win = attention(nq, _to_heads(t["nkw"], B, S, G), _to_heads(t["nvw"], B, S, G), mode="window", tq=128, tk=128,
                      window=NSA_WINDOW, gate=ng[2])
    o_c = [_from_groups(o, B, S, G, R) for o in (o_cmp, o_slc, o_win)]

    o_d = stickbreak_attention(_to_heads(t["sq"], B, S, H), _to_heads(t["sk"], B, S, H), _to_heads(t["sv"], B, S, H))
    o_d = _from_heads(o_d, B, S, H)

    z = merge([o_a, o_b, o_c[0], o_c[1], o_c[2], o_d], mg, p["w_branch"].astype(BF16))
    x = matmul(z, p["w_out"].astype(BF16), res=x)

    h2 = rmsnorm(x, p["ffn_norm"], BF16)
    pq = matmul(h2, p["peer_w_q"].astype(BF16))
    i1, i2, g = peer_topk(pq, p["peer_sub_k1"].astype(BF16), p["peer_sub_k2"].astype(BF16))

    def slots(a):
        return a.reshape(PEER_SLOTS, T).T.reshape(T, 1, PEER_SLOTS)

    w = peer_gates(slots(i1), slots(i2), slots(g))
    return peer_dense(h2, p["peer_u"].astype(BF16).T, p["peer_v"].astype(BF16), w, x)


def kernel(x, attn_norm, w_in, fox_forget_bias, mla_q_norm, mla_kv_norm, mla_w_uq, mla_w_ukv, nsa_cmp_k_pos,
           nsa_cmp_k_w1, nsa_cmp_k_w2, nsa_cmp_v_pos, nsa_cmp_v_w1, nsa_cmp_v_w2, w_branch, w_out, ffn_norm,
           peer_w_q, peer_sub_k1, peer_sub_k2, peer_u, peer_v, final_norm):
    B, S, D = x.shape
    stacked = dict(attn_norm=attn_norm, w_in=w_in, fox_forget_bias=fox_forget_bias, mla_q_norm=mla_q_norm,
                   mla_kv_norm=mla_kv_norm, mla_w_uq=mla_w_uq, mla_w_ukv=mla_w_ukv, nsa_cmp_k_pos=nsa_cmp_k_pos,
                   nsa_cmp_k_w1=nsa_cmp_k_w1, nsa_cmp_k_w2=nsa_cmp_k_w2, nsa_cmp_v_pos=nsa_cmp_v_pos,
                   nsa_cmp_v_w1=nsa_cmp_v_w1, nsa_cmp_v_w2=nsa_cmp_v_w2, w_branch=w_branch, w_out=w_out,
                   ffn_norm=ffn_norm, peer_w_q=peer_w_q, peer_sub_k1=peer_sub_k1, peer_sub_k2=peer_sub_k2,
                   peer_u=peer_u, peer_v=peer_v)
    xt = x.reshape(B * S, D)
    for l in range(w_in.shape[0]):
        xt = _layer(xt, B, S, {k: v[l] for k, v in stacked.items()})
    return rmsnorm(xt, final_norm, F32).reshape(B, S, D)
```

```python
import functools
import math

import jax
import jax.numpy as jnp
from jax import lax
from jax.experimental import pallas as pl
from jax.experimental.pallas import tpu as pltpu

F32, BF16, I32 = jnp.float32, jnp.bfloat16, jnp.int32

D_MODEL = 2048
HEAD_DIM = 64
MIX_HEADS = 8
N_BRANCH = 4
BRANCH_WIDTH = MIX_HEADS * HEAD_DIM
ROPE_THETA = 500000.0
PARTIAL_ROT = HEAD_DIM // 4
RMS_EPS = 1e-6
NEG = -1e30

MLA_Q_LORA = D_MODEL // 4
MLA_KV_LORA = D_MODEL // 8
MLA_NOPE = HEAD_DIM
MLA_ROPE = HEAD_DIM // 2
MLA_V = HEAD_DIM

NSA_KV_HEADS = 2
NSA_REP = MIX_HEADS // NSA_KV_HEADS
NSA_KV_WIDTH = NSA_KV_HEADS * HEAD_DIM
NSA_CMP_LEN = 32
NSA_CMP_STRIDE = 16
NSA_CMP_HIDDEN = 256
NSA_SEL_LEN = 64
NSA_SEL_TOP = 16
NSA_WINDOW = 512
NSA_FORCE_BONUS = 1e3

PEER_HEADS = 8
PEER_KEYS = 128
PEER_EXPERTS = PEER_KEYS * PEER_KEYS
PEER_TOPK = 16
PEER_QDIM = 256
PEER_SLOTS = PEER_HEADS * PEER_TOPK

LANES = 128
SUBLANES = 8
VMEM_LIMIT = 56 * 1024 * 1024
PAIR = LANES // HEAD_DIM

_IN_NAMES = ("fq", "fk", "fv", "ff", "mcq", "mckv", "mkr", "nq", "nkc", "nvc", "nks", "nvs", "nkw", "nvw",
             "ng", "sq", "sk", "sv", "mg")
_IN_WIDTHS = (BRANCH_WIDTH, BRANCH_WIDTH, BRANCH_WIDTH, MIX_HEADS, MLA_Q_LORA, MLA_KV_LORA, MLA_ROPE,
              BRANCH_WIDTH, NSA_KV_WIDTH, NSA_KV_WIDTH, NSA_KV_WIDTH, NSA_KV_WIDTH, NSA_KV_WIDTH, NSA_KV_WIDTH,
              3 * MIX_HEADS, BRANCH_WIDTH, BRANCH_WIDTH, BRANCH_WIDTH, N_BRANCH * D_MODEL)
_IN_SRC = {}
_off = 0
for _n, _w in zip(_IN_NAMES, _IN_WIDTHS):
    _IN_SRC[_n] = (_off, _w)
    _off += _w


def _pieces(name):
    off, w = _IN_SRC[name]
    if name == "mkr":
        half = MLA_ROPE // 2
        return [(off, half)] * MIX_HEADS + [(off + half, half)] * MIX_HEADS
    if name in ("nks", "nvs", "nkw", "nvw"):
        return [(off + g * HEAD_DIM, HEAD_DIM) for g in range(NSA_KV_HEADS) for _ in range(PAIR)]
    return [(off, w)]


_SMALL_ORDER = ("fq", "fk", "fv", "mcq", "nq", "sq", "sk", "sv", "mckv", "ff", "mkr", "nkc", "nvc", "nks", "nvs",
                "nkw", "nvw", "ng")
_SEG = {}
_off = 0
for _n in _SMALL_ORDER:
    _w = -(-sum(w for _, w in _pieces(_n)) // LANES) * LANES
    _SEG[_n] = (_off, _w)
    _off += _w
SMALL_WIDTH = -(-_off // 512) * 512


def _params(sem, vmem=None):
    return pltpu.CompilerParams(dimension_semantics=sem, vmem_limit_bytes=vmem)


def _mm(a, b):
    return jnp.dot(a, b, preferred_element_type=F32)


def _mm_nt(a, b):
    return lax.dot_general(a, b, (((1,), (1,)), ((), ())), preferred_element_type=F32)


def _split3(x):
    hi = x.astype(BF16)
    r = x - hi.astype(F32)
    mid = r.astype(BF16)
    lo = (r - mid.astype(F32)).astype(BF16)
    return hi, mid, lo


def _gelu(x):
    return 0.5 * x * (1.0 + lax.erf(x * (1.0 / math.sqrt(2.0))))


def _log_sigmoid(x):
    return jnp.minimum(x, 0.0) - jnp.log1p(jnp.exp(-jnp.abs(x)))


def _sigmoid(x):
    return 1.0 / (1.0 + jnp.exp(-x))


def _lane_col(x, idx):
    lane = lax.broadcasted_iota(I32, x.shape, 1)
    return jnp.sum(jnp.where(lane == idx, x, 0.0), axis=1, keepdims=True)


def _first_head(shape):
    return (lax.broadcasted_iota(I32, shape, 1) & (LANES - 1)) < HEAD_DIM


def _split_pair(x):
    first = _first_head(x.shape)
    zero = jnp.zeros_like(x)
    return jnp.where(first, x, zero), jnp.where(first, zero, x)


def _rmsnorm_kernel(x_ref, g_ref, o_ref):
    x = x_ref[...]
    y = x * lax.rsqrt(jnp.mean(x * x, axis=-1, keepdims=True) + RMS_EPS)
    o_ref[...] = (y * g_ref[...]).astype(o_ref.dtype)


def rmsnorm(x, g, out_dtype, tm=256):
    T, D = x.shape
    return pl.pallas_call(
        _rmsnorm_kernel,
        grid=(T // tm,),
        in_specs=[pl.BlockSpec((tm, D), lambda i: (i, 0)), pl.BlockSpec((1, D), lambda i: (0, 0))],
        out_specs=pl.BlockSpec((tm, D), lambda i: (i, 0)),
        out_shape=jax.ShapeDtypeStruct((T, D), out_dtype),
        compiler_params=_params(("parallel",)),
        name="rmsnorm",
    )(x, g.reshape(1, D))


def _matmul_kernel(a_ref, b_ref, *rest):
    o_ref = rest[-1]
    acc = _mm(a_ref[...], b_ref[...])
    if len(rest) == 2:
        acc = acc + rest[0][...]
    o_ref[...] = acc.astype(o_ref.dtype)


def matmul(a, b, res=None, out_dtype=F32, tm=1024, tn=512):
    M, K = a.shape
    N = b.shape[1]
    tm, tn = min(tm, M), min(tn, N)
    in_specs = [pl.BlockSpec((tm, K), lambda j, i: (i, 0)), pl.BlockSpec((K, tn), lambda j, i: (0, j))]
    args = [a, b]
    if res is not None:
        in_specs.append(pl.BlockSpec((tm, tn), lambda j, i: (i, j)))
        args.append(res)
    return pl.pallas_call(
        _matmul_kernel,
        grid=(N // tn, M // tm),
        in_specs=in_specs,
        out_specs=pl.BlockSpec((tm, tn), lambda j, i: (i, j)),
        out_shape=jax.ShapeDtypeStruct((M, N), out_dtype),
        compiler_params=_params(("parallel", "parallel"), VMEM_LIMIT),
        name="matmul",
    )(*args)


def _rope_roll(x, cos, sin, sh, period):
    w = x.shape[-1]
    lane = lax.broadcasted_iota(I32, x.shape, 1) & (period - 1)
    sa = jnp.where(lane < sh, -sin, 0.0)
    sb = jnp.where((lane >= sh) & (lane < 2 * sh), sin, 0.0)
    x_up = pltpu.roll(x, w - sh, 1)
    x_dn = pltpu.roll(x, sh, 1)
    return x * cos + x_up * sa + x_dn * sb


def _rope_halves(x, cos, sin):
    h = x.shape[-1] // 2
    x1, x2 = x[:, :h], x[:, h:]
    return jnp.concatenate([x1 * cos - x2 * sin, x1 * sin + x2 * cos], axis=1)


def _prep_kernel(ps_ref, fb_ref, gq_ref, gkv_ref, wuq_ref, wukv_ref, ncos_ref, nsin_ref, mcos_ref, msin_ref,
                 fq_o, fk_o, fv_o, lf_o, qn_o, qr_o, kn_o, mv_o, kr_o,
                 nq_o, nkc_o, nvc_o, nks_o, nvs_o, nkw_o, nvw_o, ng_o, sq_o, sk_o, sv_o):
    def seg(name):
        off, w = _SEG[name]
        return ps_ref[:, off:off + w]

    qs = HEAD_DIM ** -0.5
    fq_o[...] = (seg("fq") * qs).astype(BF16)
    fk_o[...] = seg("fk").astype(BF16)
    fv_o[...] = seg("fv").astype(BF16)
    lf_o[...] = _log_sigmoid(seg("ff") + fb_ref[...])
    sq_o[...] = (seg("sq") * qs).astype(BF16)
    sk_o[...] = seg("sk").astype(BF16)
    sv_o[...] = seg("sv").astype(BF16)
    cq = seg("mcq")
    hq = cq * lax.rsqrt(jnp.mean(cq * cq, axis=-1, keepdims=True) + RMS_EPS) * gq_ref[...]
    qa = _mm(hq.astype(BF16), wuq_ref[...]) * (MLA_NOPE + MLA_ROPE) ** -0.5
    mcos, msin = mcos_ref[...], msin_ref[...]
    qn_o[...] = qa[:, :BRANCH_WIDTH].astype(BF16)
    qr_o[...] = _rope_halves(qa[:, BRANCH_WIDTH:], mcos, msin).astype(BF16)
    ckv = seg("mckv")
    hkv = ckv * lax.rsqrt(jnp.mean(ckv * ckv, axis=-1, keepdims=True) + RMS_EPS) * gkv_ref[...]
    kva = _mm(hkv.astype(BF16), wukv_ref[...])
    kn_o[...] = kva[:, :BRANCH_WIDTH].astype(BF16)
    mv_o[...] = kva[:, BRANCH_WIDTH:].astype(BF16)
    kr_o[...] = _rope_halves(seg("mkr"), mcos, msin).astype(BF16)
    ncos, nsin = ncos_ref[...], nsin_ref[...]
    kw = _SEG["nks"][1]
    kcos, ksin = ncos[:, :kw], nsin[:, :kw]
    nq_o[...] = (_rope_roll(seg("nq"), ncos, nsin, PARTIAL_ROT // 2, HEAD_DIM) * qs).astype(BF16)
    nks_o[...] = _rope_roll(seg("nks"), kcos, ksin, PARTIAL_ROT // 2, HEAD_DIM).astype(BF16)
    nkw_o[...] = _rope_roll(seg("nkw"), kcos, ksin, PARTIAL_ROT // 2, HEAD_DIM).astype(BF16)
    nkc_o[...] = seg("nkc").astype(BF16)
    nvc_o[...] = seg("nvc").astype(BF16)
    nvs_o[...] = seg("nvs").astype(BF16)
    nvw_o[...] = seg("nvw").astype(BF16)
    ng_o[...] = seg("ng")


def prep(ps, fb, gq, gkv, wuq, wukv, ncos, nsin, mcos, msin, S, tm=256):
    T = ps.shape[0]
    n_pos = S // tm

    def tok(w):
        return pl.BlockSpec((tm, w), lambda i: (i, 0))

    def const(shape):
        return pl.BlockSpec(shape, lambda i: (0, 0))

    def pos(w):
        return pl.BlockSpec((tm, w), lambda i: (i % n_pos, 0))

    outs = [("fq", 512, BF16), ("fk", 512, BF16), ("fv", 512, BF16), ("lf", 128, F32),
            ("qn", 512, BF16), ("qr", 256, BF16), ("kn", 512, BF16), ("mv", 512, BF16), ("kr", 256, BF16),
            ("nq", 512, BF16), ("nkc", 128, BF16), ("nvc", 128, BF16), ("nks", 256, BF16), ("nvs", 256, BF16),
            ("nkw", 256, BF16), ("nvw", 256, BF16), ("ng", 128, F32),
            ("sq", 512, BF16), ("sk", 512, BF16), ("sv", 512, BF16)]
    res = pl.pallas_call(
        _prep_kernel,
        grid=(T // tm,),
        in_specs=[tok(SMALL_WIDTH), const(fb.shape), const(gq.shape), const(gkv.shape), const(wuq.shape),
                  const(wukv.shape), pos(512), pos(512), pos(128), pos(128)],
        out_specs=[tok(w) for _, w, _ in outs],
        out_shape=[jax.ShapeDtypeStruct((T, w), dt) for _, w, dt in outs],
        compiler_params=_params(("parallel",), VMEM_LIMIT),
        name="prep",
    )(ps, fb, gq, gkv, wuq, wukv, ncos, nsin, mcos, msin)
    return {n: r for (n, _, _), r in zip(outs, res)}


def _cumsum_kernel(x_ref, o_ref, *, S, ch):
    r = lax.broadcasted_iota(I32, (ch, ch), 0)
    c = lax.broadcasted_iota(I32, (ch, ch), 1)
    tri = jnp.where(c <= r, 1.0, 0.0).astype(BF16)
    carry = jnp.zeros((1, x_ref.shape[-1]), F32)
    for i in range(S // ch):
        hi, mid, lo = _split3(x_ref[i * ch:(i + 1) * ch, :])
        y = _mm(tri, hi) + _mm(tri, mid) + _mm(tri, lo) + carry
        o_ref[i * ch:(i + 1) * ch, :] = y
        carry = y[ch - 1:ch, :]


def cumsum_tokens(x, S, ch=256):
    T, W = x.shape
    return pl.pallas_call(
        functools.partial(_cumsum_kernel, S=S, ch=ch),
        grid=(T // S,),
        in_specs=[pl.BlockSpec((S, W), lambda b: (b, 0))],
        out_specs=pl.BlockSpec((S, W), lambda b: (b, 0)),
        out_shape=jax.ShapeDtypeStruct((T, W), F32),
        compiler_params=_params(("parallel",)),
        name="fox_cumsum",
    )(x)


def _value_pair(v):
    first = _first_head(v.shape)
    one = jnp.ones_like(v)
    return jnp.where(first, v, one), jnp.where(first, one, v)


def _softmax_tile(s, mask, m, acc, v):
    if mask is not None:
        s = jnp.where(mask, s, NEG)
    m_new = jnp.maximum(m, jnp.max(s, axis=-1, keepdims=True))
    p = jnp.exp(s - m_new)
    if mask is not None:
        p = jnp.where(mask, p, 0.0)
    return m_new, jnp.exp(m - m_new) * acc + _mm(p.astype(BF16), v)


def _softmax_init(t, n):
    return tuple((jnp.full((t, 1), NEG, F32), jnp.zeros((t, LANES), F32)) for _ in range(n))


def _normalize_pair(acc_a, acc_b):
    first = _first_head(acc_a.shape)
    return jnp.where(first, acc_a / acc_a[:, HEAD_DIM:HEAD_DIM + 1], acc_b / acc_b[:, 0:1])


def _causal_attn_kernel(*refs, t, fox, mla):
    if mla:
        q_ref, qr_ref, k_ref, kr_ref, v_ref, o_ref = refs
    elif fox:
        q_ref, k_ref, v_ref, fcol_ref, frow_ref, o_ref = refs
    hp, qi = pl.program_id(1), pl.program_id(2)
    heads = (PAIR * hp, PAIR * hp + 1)
    qs = _split_pair(q_ref[...])
    if mla:
        qr = qr_ref[...]
        lane_head = lax.shift_right_logical(lax.broadcasted_iota(I32, qr.shape, 1) & (LANES - 1),
                                            int(math.log2(MLA_ROPE // 2)))
        qs = tuple(jnp.concatenate([q, jnp.where(lane_head == h, qr, jnp.zeros_like(qr))], axis=1)
                   for q, h in zip(qs, heads))
    if fox:
        fcol = fcol_ref[...]
        fq = tuple(_lane_col(fcol, h) for h in heads)
    qpos = qi * t + lax.broadcasted_iota(I32, (t, 1), 0)

    def tile(j, carry, diag):
        k0 = pl.multiple_of(j * t, t)
        k = k_ref[pl.ds(k0, t), :]
        if mla:
            k = jnp.concatenate([k, kr_ref[pl.ds(k0, t), :]], axis=1)
        vs = _value_pair(v_ref[pl.ds(k0, t), :])
        mask = (k0 + lax.broadcasted_iota(I32, (1, t), 1)) <= qpos if diag else None
        new = []
        for a in range(PAIR):
            s = _mm_nt(qs[a], k)
            if fox:
                s = s + fq[a] - frow_ref[0, pl.ds(heads[a], 1), pl.ds(k0, t)]
            new.append(_softmax_tile(s, mask, *carry[a], vs[a]))
        return tuple(new)

    carry = lax.fori_loop(0, qi, lambda j, c: tile(j, c, False), _softmax_init(t, PAIR))
    (_, acc_a), (_, acc_b) = tile(qi, carry, True)
    o_ref[...] = _normalize_pair(acc_a, acc_b).astype(o_ref.dtype)


def causal_attention(q, k, v, B, S, *, qr=None, kr=None, fcol=None, frow=None, t=256):
    T, W = q.shape
    nq = S // t
    mla, fox = qr is not None, fcol is not None

    def qspec(w, col):
        return pl.BlockSpec((t, w), (lambda b, h, i: (b * nq + i, h)) if col else (lambda b, h, i: (b * nq + i, 0)))

    def kspec(w, col):
        return pl.BlockSpec((S, w), (lambda b, h, i: (b, h)) if col else (lambda b, h, i: (b, 0)))

    if mla:
        in_specs = [qspec(LANES, True), qspec(qr.shape[1], False), kspec(LANES, True), kspec(kr.shape[1], False),
                    kspec(LANES, True)]
        args = [q, qr, k, kr, v]
    else:
        in_specs = [qspec(LANES, True), kspec(LANES, True), kspec(LANES, True), qspec(LANES, False),
                    pl.BlockSpec((1,) + frow.shape[1:], lambda b, h, i: (b, 0, 0))]
        args = [q, k, v, fcol, frow]
    return pl.pallas_call(
        functools.partial(_causal_attn_kernel, t=t, fox=fox, mla=mla),
        grid=(B, W // LANES, nq),
        in_specs=in_specs,
        out_specs=qspec(LANES, True),
        out_shape=jax.ShapeDtypeStruct((T, W), BF16),
        compiler_params=_params(("parallel", "parallel", "parallel"), VMEM_LIMIT),
        name="attn_mla" if mla else "attn_fox",
    )(*args)


def _nsa_gates(ng, g, branch):
    return [_sigmoid(_lane_col(ng, (g * NSA_REP + h) * 3 + branch)) for h in range(NSA_REP)]


def _nsa_queries(q):
    qs = []
    for blk in range(q.shape[1] // LANES):
        qs.extend(_split_pair(q[:, blk * LANES:(blk + 1) * LANES]))
    return qs


def _nsa_attn_kernel(*refs, t, mode, window, branch):
    if mode == "select":
        q_ref, k_ref, v_ref, ng_ref, sel_ref, emat_ref, o_ref = refs
        sel = sel_ref[0]
    else:
        q_ref, k_ref, v_ref, ng_ref, o_ref = refs
    g, qi = pl.program_id(1), pl.program_id(2)
    qs = _nsa_queries(q_ref[...])
    gates = _nsa_gates(ng_ref[...], g, branch)
    qpos = qi * t + lax.broadcasted_iota(I32, (t, 1), 0)

    def tile(j, carry, diag):
        k0 = pl.multiple_of(j * t, t)
        k = k_ref[pl.ds(k0, t), :]
        vs = _value_pair(v_ref[pl.ds(k0, t), :])
        kpos = k0 + lax.broadcasted_iota(I32, (1, t), 1)
        if mode == "select":
            mask = _mm(sel, emat_ref[:, pl.ds(k0, t)]) > 0.5
            if diag:
                mask = mask & (kpos <= qpos)
        else:
            mask = (kpos <= qpos) & (kpos > qpos - window)
        return tuple(_softmax_tile(_mm_nt(qs[h], k), mask, *carry[h], vs[h % PAIR]) for h in range(NSA_REP))

    init = _softmax_init(t, NSA_REP)
    if mode == "select":
        carry = lax.fori_loop(0, qi, lambda j, c: tile(j, c, False), init)
        carry = tile(qi, carry, True)
    else:
        carry = lax.fori_loop(jnp.maximum(qi - window // t, 0), qi + 1, lambda j, c: tile(j, c, True), init)
    first = _first_head((t, LANES))
    blocks = []
    for blk in range(NSA_REP // PAIR):
        (_, acc_a), (_, acc_b) = carry[PAIR * blk], carry[PAIR * blk + 1]
        blocks.append(jnp.where(first, acc_a / acc_a[:, HEAD_DIM:HEAD_DIM + 1] * gates[PAIR * blk],
                                acc_b / acc_b[:, 0:1] * gates[PAIR * blk + 1]))
    o_ref[...] = jnp.concatenate(blocks, axis=1)


def nsa_attention(q, k, v, ng, B, S, *, mode, branch, sel=None, emat=None, window=None, t=256):
    T, W = q.shape
    nq = S // t
    gw = NSA_REP * HEAD_DIM

    def qspec(w, col):
        return pl.BlockSpec((t, w), (lambda b, g, i: (b * nq + i, g)) if col else (lambda b, g, i: (b * nq + i, 0)))

    kspec = pl.BlockSpec((S, LANES), lambda b, g, i: (b, g))
    in_specs = [qspec(gw, True), kspec, kspec, qspec(LANES, False)]
    args = [q, k, v, ng]
    if mode == "select":
        in_specs += [pl.BlockSpec((1, t, sel.shape[-1]), lambda b, g, i: (g, b * nq + i, 0)),
                     pl.BlockSpec(emat.shape, lambda b, g, i: (0, 0))]
        args += [sel, emat]
    return pl.pallas_call(
        functools.partial(_nsa_attn_kernel, t=t, mode=mode, window=window, branch=branch),
        grid=(B, W // gw, nq),
        in_specs=in_specs,
        out_specs=qspec(gw, True),
        out_shape=jax.ShapeDtypeStruct((T, W), F32),
        compiler_params=_params(("parallel", "parallel", "parallel"), VMEM_LIMIT),
        name="nsa_" + mode,
    )(*args)


def _stickbreak_kernel(q_ref, k_ref, v_ref, o_ref, *, t):
    qi = pl.program_id(2)
    qs = _split_pair(q_ref[...])
    qpos = qi * t + lax.broadcasted_iota(I32, (t, 1), 0)
    r = lax.broadcasted_iota(I32, (t, t), 0)
    c = lax.broadcasted_iota(I32, (t, t), 1)
    from_here = jnp.where(r >= c, 1.0, 0.0).astype(BF16)

    def tile(j, carry, diag):
        k0 = pl.multiple_of(j * t, t)
        k = k_ref[pl.ds(k0, t), :]
        v = v_ref[pl.ds(k0, t), :]
        mask = (k0 + lax.broadcasted_iota(I32, (1, t), 1)) < qpos if diag else None
        new = []
        for a in range(PAIR):
            later, acc = carry[a]
            z = _mm_nt(qs[a], k)
            sp = jnp.maximum(z, 0.0) + jnp.log(1.0 + jnp.exp(-jnp.abs(z)))
            if diag:
                sp = jnp.where(mask, sp, 0.0)
            hi = sp.astype(BF16)
            lo = (sp - hi.astype(F32)).astype(BF16)
            cum = _mm(hi, from_here) + _mm(lo, from_here)
            w = jnp.exp(z - cum - later)
            if diag:
                w = jnp.where(mask, w, 0.0)
            new.append((later + cum[:, 0:1], acc + _mm(w.astype(BF16), v)))
        return tuple(new)

    init = tuple((jnp.zeros((t, 1), F32), jnp.zeros((t, LANES), F32)) for _ in range(PAIR))
    carry = tile(qi, init, True)
    (_, acc_a), (_, acc_b) = lax.fori_loop(0, qi, lambda jj, cr: tile(qi - 1 - jj, cr, False), carry)
    o_ref[...] = jnp.where(_first_head(acc_a.shape), acc_a, acc_b).astype(o_ref.dtype)


def stickbreak_attention(q, k, v, B, S, t=256):
    T, W = q.shape
    nq = S // t
    qspec = pl.BlockSpec((t, LANES), lambda b, h, i: (b * nq + i, h))
    kspec = pl.BlockSpec((S, LANES), lambda b, h, i: (b, h))
    return pl.pallas_call(
        functools.partial(_stickbreak_kernel, t=t),
        grid=(B, W // LANES, nq),
        in_specs=[qspec, kspec, kspec],
        out_specs=qspec,
        out_shape=jax.ShapeDtypeStruct((T, W), BF16),
        compiler_params=_params(("parallel", "parallel", "parallel"), VMEM_LIMIT),
        name="stickbreak",
    )(q, k, v)


def _compress_kernel(u_ref, pos_ref, w1_ref, w2_ref, w2s_ref, cos_ref, sin_ref, o_ref, *, rotary):
    u = u_ref[0]
    half = u.shape[-1]
    n = u.shape[0]
    first = _mm(u, w1_ref[:half, :])
    second = _mm(u, w1_ref[half:, :])
    bias = _mm(pos_ref[...], w1_ref[...])[0:1, :]
    hid = _gelu(first + pltpu.roll(second, n - 1, 0) + bias).astype(BF16)
    out = _mm(hid, w2_ref[...])
    if rotary:
        out = out * cos_ref[...] + _mm(hid, w2s_ref[...]) * sin_ref[...]
    o_ref[0] = jnp.concatenate([out] * PAIR, axis=1).astype(o_ref.dtype)


def compress(u, pos8, w1, w2, w2s, cos, sin, rotary):
    N, n, half = u.shape

    def const(a):
        return pl.BlockSpec(a.shape, lambda i: (0,) * a.ndim)

    return pl.pallas_call(
        functools.partial(_compress_kernel, rotary=rotary),
        grid=(N,),
        in_specs=[pl.BlockSpec((1, n, half), lambda i: (i, 0, 0)), const(pos8), const(w1), const(w2), const(w2s),
                  const(cos), const(sin)],
        out_specs=pl.BlockSpec((1, n, LANES), lambda i: (i, 0, 0)),
        out_shape=jax.ShapeDtypeStruct((N, n, LANES), BF16),
        compiler_params=_params(("parallel",)),
        name="nsa_compress",
    )(u, pos8, w1, w2, w2s, cos, sin)


def _cmp_attn_kernel(q_ref, kc_ref, vc_ref, ng_ref, c2s_ref, o_ref, sel_ref, *, t, n_cmp, n_top):
    g, qi = pl.program_id(1), pl.program_id(2)
    kc, vc = kc_ref[0], vc_ref[0]
    nc = kc.shape[0]
    n_sel = c2s_ref.shape[-1]
    qs = _nsa_queries(q_ref[...])
    gates = _nsa_gates(ng_ref[...], g, 0)
    qpos = qi * t + lax.broadcasted_iota(I32, (t, 1), 0)
    cidx = lax.broadcasted_iota(I32, (1, nc), 1)
    cmask = (cidx * NSA_CMP_STRIDE + (NSA_CMP_LEN - 1) <= qpos) & (cidx < n_cmp)
    outs, psum = [], None
    for h in range(NSA_REP):
        s = jnp.where(cmask, _mm_nt(qs[h], kc), NEG)
        e = jnp.where(cmask, jnp.exp(s - jnp.max(s, axis=-1, keepdims=True)), 0.0)
        den = jnp.sum(e, axis=-1, keepdims=True)
        p = e / jnp.where(den > 0.0, den, 1.0)
        outs.append(_mm(p.astype(BF16), vc) * gates[h])
        psum = p if psum is None else psum + p
    first = _first_head((t, LANES))
    o_ref[...] = jnp.concatenate([jnp.where(first, outs[PAIR * b], outs[PAIR * b + 1])
                                  for b in range(NSA_REP // PAIR)], axis=1)

    hi, mid, lo = _split3(psum)
    c2s = c2s_ref[...]
    imp = _mm(hi, c2s) + _mm(mid, c2s) + _mm(lo, c2s)
    blk = lax.broadcasted_iota(I32, (1, n_sel), 1)
    cur = lax.shift_right_logical(qpos, int(math.log2(NSA_SEL_LEN)))
    forced = ((blk == 0) | (blk == cur) | (blk == cur - 1)).astype(F32)
    valid = blk * NSA_SEL_LEN <= qpos
    score = jnp.where(valid, imp + NSA_FORCE_BONUS * forced, NEG)
    rank = jnp.zeros((t, n_sel), F32)
    for i in range(n_sel):
        si = score[:, i:i + 1]
        ahead = (si > score) | ((si == score) & (blk > i))
        rank = rank + jnp.where(ahead, 1.0, 0.0)
    sel_ref[0] = jnp.where((rank < n_top) & valid, 1.0, 0.0).astype(sel_ref.dtype)


def cmp_attention(q, kc, vc, ng, c2s, B, S, n_cmp, n_top, t=128):
    T, W = q.shape
    nq = S // t
    gw = NSA_REP * HEAD_DIM
    G = W // gw
    n_sel = c2s.shape[-1]

    def qspec(w, col):
        return pl.BlockSpec((t, w), (lambda b, g, i: (b * nq + i, g)) if col else (lambda b, g, i: (b * nq + i, 0)))

    cspec = pl.BlockSpec((1,) + kc.shape[1:], lambda b, g, i: (b * G + g, 0, 0))
    return pl.pallas_call(
        functools.partial(_cmp_attn_kernel, t=t, n_cmp=n_cmp, n_top=n_top),
        grid=(B, G, nq),
        in_specs=[qspec(gw, True), cspec, cspec, qspec(LANES, False), pl.BlockSpec(c2s.shape, lambda b, g, i: (0, 0))],
        out_specs=[qspec(gw, True), pl.BlockSpec((1, t, n_sel), lambda b, g, i: (g, b * nq + i, 0))],
        out_shape=[jax.ShapeDtypeStruct((T, W), F32), jax.ShapeDtypeStruct((G, T, n_sel), BF16)],
        compiler_params=_params(("parallel", "parallel", "parallel")),
        name="nsa_cmp_attn",
    )(q, kc, vc, ng, c2s)


def _merge_kernel(oa_ref, ob_ref, oc1_ref, oc2_ref, oc3_ref, od_ref, g0_ref, g1_ref, g2_ref, g3_ref, wb_ref, z_ref):
    oc = (oc1_ref[...] + oc2_ref[...] + oc3_ref[...]).astype(BF16)
    branches = (oa_ref[...], ob_ref[...], oc, od_ref[...])
    gates = (g0_ref, g1_ref, g2_ref, g3_ref)
    z = None
    for n in range(N_BRANCH):
        y = _sigmoid(gates[n][...]) * _mm(branches[n], wb_ref[n])
        z = y if z is None else z + y
    z_ref[...] = z.astype(z_ref.dtype)


def merge(outs, mg, wb, tm=512, tn=512):
    T = mg.shape[0]
    D = wb.shape[-1]
    nj = D // tn
    ospec = pl.BlockSpec((tm, BRANCH_WIDTH), lambda i, j: (i, 0))
    gspecs = [pl.BlockSpec((tm, tn), functools.partial(lambda i, j, n: (i, n * nj + j), n=n)) for n in range(N_BRANCH)]
    return pl.pallas_call(
        _merge_kernel,
        grid=(T // tm, nj),
        in_specs=[ospec] * 6 + gspecs + [pl.BlockSpec((N_BRANCH, BRANCH_WIDTH, tn), lambda i, j: (0, 0, j))],
        out_specs=pl.BlockSpec((tm, tn), lambda i, j: (i, j)),
        out_shape=jax.ShapeDtypeStruct((T, D), BF16),
        compiler_params=_params(("parallel", "parallel"), VMEM_LIMIT),
        name="merge",
    )(*outs, mg, mg, mg, mg, wb)


def _topk_rows(s, k, payload=None):
    n = s.shape[0]
    row = lax.broadcasted_iota(I32, s.shape, 0).astype(F32)
    vals, rows, pays = [], [], []
    for _ in range(k):
        m = jnp.max(s, axis=0, keepdims=True)
        am = jnp.min(jnp.where(s == m, row, float(n)), axis=0, keepdims=True)
        hit = row == am
        vals.append(m)
        rows.append(am)
        if payload is not None:
            pays.append(jnp.sum(jnp.where(hit, payload, 0.0), axis=0, keepdims=True))
        s = jnp.where(hit, -jnp.inf, s)
    vals, rows = jnp.concatenate(vals, axis=0), jnp.concatenate(rows, axis=0)
    return (vals, rows) if payload is None else (vals, rows, jnp.concatenate(pays, axis=0))


def _peer_topk_kernel(q_ref, k1_ref, k2_ref, i1_ref, i2_ref, g_ref):
    q = q_ref[...].astype(BF16)
    half = PEER_QDIM // 2
    v1, a1 = _topk_rows(_mm_nt(k1_ref[0], q[:, :half]), PEER_TOPK)
    v2, a2 = _topk_rows(_mm_nt(k2_ref[0], q[:, half:]), PEER_TOPK)
    cand = jnp.concatenate([v1[a:a + 1] + v2 for a in range(PEER_TOPK)], axis=0)
    cidx = jnp.concatenate([a1[a:a + 1] * float(PEER_KEYS) + a2 for a in range(PEER_TOPK)], axis=0)
    top_s, _, eidx = _topk_rows(cand, PEER_TOPK, payload=cidx)
    ex = jnp.exp(top_s - top_s[0:1])
    g_ref[0] = ex / jnp.sum(ex, axis=0, keepdims=True)
    e = eidx.astype(I32)
    i1_ref[0] = lax.shift_right_logical(e, int(math.log2(PEER_KEYS)))
    i2_ref[0] = e & (PEER_KEYS - 1)


def peer_topk(q, k1, k2, tm=256):
    T = q.shape[0]
    H = k1.shape[0]
    ospec = pl.BlockSpec((1, PEER_TOPK, tm), lambda i, h: (h, 0, i))
    kspec = pl.BlockSpec((1,) + k1.shape[1:], lambda i, h: (h, 0, 0))
    return pl.pallas_call(
        _peer_topk_kernel,
        grid=(T // tm, H),
        in_specs=[pl.BlockSpec((tm, PEER_QDIM), lambda i, h: (i, h)), kspec, kspec],
        out_specs=[ospec, ospec, ospec],
        out_shape=[jax.ShapeDtypeStruct((H, PEER_TOPK, T), I32), jax.ShapeDtypeStruct((H, PEER_TOPK, T), I32),
                   jax.ShapeDtypeStruct((H, PEER_TOPK, T), F32)],
        compiler_params=_params(("parallel", "parallel")),
        name="peer_topk",
    )(q, k1, k2)


def _peer_gate_kernel(i1_ref, i2_ref, g_ref, o_ref, *, tm, nr):
    i1, i2, g = i1_ref[...], i2_ref[...], g_ref[...]
    n = PEER_KEYS
    g_hi = g.astype(BF16).astype(F32)
    g_lo = g - g_hi
    row = lax.broadcasted_iota(I32, (tm, n, PEER_SLOTS), 1)
    first = i1 == row
    a = jnp.concatenate([jnp.where(first, g_hi, 0.0).astype(BF16), jnp.where(first, g_lo, 0.0).astype(BF16)], axis=-1)
    second = jnp.where(i2 == row, 1.0, 0.0).astype(BF16)
    b = jnp.concatenate([second, second], axis=-1)
    w = lax.dot_general(a, b, (((2,), (2,)), ((0,), (0,))), preferred_element_type=F32)
    for ch in range(n // nr):
        o_ref[ch] = w[:, ch * nr:(ch + 1) * nr, :].reshape(tm * nr, n)


def peer_gates(i1, i2, g, nr=SUBLANES, tm=32):
    T = i1.shape[0]
    n = PEER_KEYS
    spec = pl.BlockSpec((tm, 1, PEER_SLOTS), lambda i: (i, 0, 0))
    return pl.pallas_call(
        functools.partial(_peer_gate_kernel, tm=tm, nr=nr),
        grid=(T // tm,),
        in_specs=[spec, spec, spec],
        out_specs=pl.BlockSpec((n // nr, tm * nr, n), lambda i: (0, i, 0)),
        out_shape=jax.ShapeDtypeStruct((n // nr, T * nr, n), F32),
        compiler_params=_params(("parallel",), VMEM_LIMIT),
        name="peer_gates",
    )(i1, i2, g)


def _peer_dense_kernel(h_ref, ut_ref, v_ref, w_ref, res_ref, o_ref, *, tm, nr):
    @pl.when(pl.program_id(1) == 0)
    def _():
        o_ref[...] = res_ref[...]

    act = _gelu(_mm(h_ref[...], ut_ref[...]))
    n = PEER_KEYS
    cols = []
    for r in range(nr):
        w_r = w_ref[pl.ds(r, tm, stride=nr), :]
        cols.append((w_r * act[:, r * n:(r + 1) * n]).astype(BF16))
    o_ref[...] += _mm(jnp.concatenate(cols, axis=1), v_ref[...])


def peer_dense(h, ut, v, w, res, nr=SUBLANES, tm=512):
    T, D = h.shape
    E = v.shape[0]
    te = nr * PEER_KEYS
    return pl.pallas_call(
        functools.partial(_peer_dense_kernel, tm=tm, nr=nr),
        grid=(T // tm, E // te),
        in_specs=[pl.BlockSpec((tm, D), lambda i, e: (i, 0)),
                  pl.BlockSpec((D, te), lambda i, e: (0, e)),
                  pl.BlockSpec((te, D), lambda i, e: (e, 0)),
                  pl.BlockSpec((None, tm * nr, PEER_KEYS), lambda i, e: (e, i, 0)),
                  pl.BlockSpec((tm, D), lambda i, e: (i, 0))],
        out_specs=pl.BlockSpec((tm, D), lambda i, e: (i, 0)),
        out_shape=jax.ShapeDtypeStruct((T, D), F32),
        compiler_params=_params(("parallel", "arbitrary"), VMEM_LIMIT),
        name="peer_dense",
    )(h, ut, v, w, res)


def _rope_tables(pos, half, reps, width):
    inv = jnp.exp(-math.log(ROPE_THETA) * jnp.arange(half, dtype=F32) / half)
    ang = pos.astype(F32)[:, None] * inv[None, :]
    pad = width - 2 * half
    cos = jnp.concatenate([jnp.cos(ang), jnp.cos(ang), jnp.ones((pos.shape[0], pad), F32)], axis=-1)
    sin = jnp.concatenate([jnp.sin(ang), jnp.sin(ang), jnp.zeros((pos.shape[0], pad), F32)], axis=-1)
    return jnp.tile(cos, (1, reps)), jnp.tile(sin, (1, reps))


def _small_weight(w_in_l):
    cols, used = [], 0
    for name in _SMALL_ORDER:
        for off, w in _pieces(name):
            cols.append(w_in_l[:, off:off + w])
            used += w
        end = _SEG[name][0] + _SEG[name][1]
        if end > used:
            cols.append(jnp.zeros((w_in_l.shape[0], end - used), w_in_l.dtype))
            used = end
    if SMALL_WIDTH > used:
        cols.append(jnp.zeros((w_in_l.shape[0], SMALL_WIDTH - used), w_in_l.dtype))
    return jnp.concatenate(cols, axis=1).astype(BF16)


def _layer(x, B, S, p):
    T = B * S
    H, G = MIX_HEADS, NSA_KV_HEADS
    pos = jnp.arange(S)

    h = rmsnorm(x, p["attn_norm"], BF16)
    ps = matmul(h, _small_weight(p["w_in"]))
    mg_off, mg_w = _IN_SRC["mg"]
    mg = matmul(h, p["w_in"][:, mg_off:mg_off + mg_w].astype(BF16))

    wuq = p["mla_w_uq"].reshape(MLA_Q_LORA, H, MLA_NOPE + MLA_ROPE)
    r_half = MLA_ROPE // 2
    wuq = jnp.concatenate([wuq[:, :, :MLA_NOPE].reshape(MLA_Q_LORA, -1),
                           wuq[:, :, MLA_NOPE:MLA_NOPE + r_half].reshape(MLA_Q_LORA, -1),
                           wuq[:, :, MLA_NOPE + r_half:].reshape(MLA_Q_LORA, -1)], axis=1).astype(BF16)
    wukv = p["mla_w_ukv"].reshape(MLA_KV_LORA, H, MLA_NOPE + MLA_V)
    wukv = jnp.concatenate([wukv[:, :, :MLA_NOPE].reshape(MLA_KV_LORA, -1),
                            wukv[:, :, MLA_NOPE:].reshape(MLA_KV_LORA, -1)], axis=1).astype(BF16)
    ncos, nsin = _rope_tables(pos, PARTIAL_ROT // 2, H, HEAD_DIM)
    mcos, msin = _rope_tables(pos, r_half, LANES // MLA_ROPE, MLA_ROPE)
    fb = jnp.pad(p["fox_forget_bias"], (0, LANES - H)).reshape(1, LANES)
    t = prep(ps, fb, p["mla_q_norm"].reshape(1, -1), p["mla_kv_norm"].reshape(1, -1), wuq, wukv,
             ncos, nsin, mcos, msin, S)

    fcum = cumsum_tokens(t["lf"], S)
    frow = fcum[:, :H].reshape(B, S, H).transpose(0, 2, 1)
    o_a = causal_attention(t["fq"], t["fk"], t["fv"], B, S, fcol=fcum, frow=frow)

    o_b = causal_attention(t["qn"], t["kn"], t["mv"], B, S, qr=t["qr"], kr=t["kr"])

    n_half = S // NSA_CMP_STRIDE
    n_cmp = (S - NSA_CMP_LEN) // NSA_CMP_STRIDE + 1
    n_sel = S // NSA_SEL_LEN
    cmp_end = jnp.arange(n_half) * NSA_CMP_STRIDE + NSA_CMP_LEN - 1
    ccos, csin = _rope_tables(cmp_end, PARTIAL_ROT // 2, 1, HEAD_DIM)
    d = jnp.arange(HEAD_DIM)
    csin = jnp.where(d < PARTIAL_ROT // 2, -csin, csin)
    swap = jnp.where(d < PARTIAL_ROT // 2, d + PARTIAL_ROT // 2, d - PARTIAL_ROT // 2)

    def compressed(tok, pos_emb, w1, w2, rotary):
        u = tok.reshape(B, S, G, HEAD_DIM).transpose(0, 2, 1, 3).reshape(B * G, n_half, NSA_CMP_STRIDE * HEAD_DIM)
        pos8 = jnp.pad(pos_emb.reshape(1, -1), ((0, SUBLANES - 1), (0, 0))).astype(BF16)
        w2s = jnp.where(d[None, :] < PARTIAL_ROT, w2[:, swap], 0.0)
        return compress(u, pos8, w1.astype(BF16), w2.astype(BF16), w2s.astype(BF16), ccos, csin, rotary)

    kc = compressed(t["nkc"], p["nsa_cmp_k_pos"], p["nsa_cmp_k_w1"], p["nsa_cmp_k_w2"], True)
    vc = compressed(t["nvc"], p["nsa_cmp_v_pos"], p["nsa_cmp_v_w1"], p["nsa_cmp_v_w2"], False)
    cmp_start = jnp.arange(n_half) * NSA_CMP_STRIDE
    sel_lo = jnp.arange(n_sel) * NSA_SEL_LEN
    cover = jnp.clip(jnp.minimum(cmp_start[:, None] + NSA_CMP_LEN, sel_lo[None, :] + NSA_SEL_LEN)
                     - jnp.maximum(cmp_start[:, None], sel_lo[None, :]), 0, None)
    c2s = (cover.astype(F32) / NSA_CMP_LEN).astype(BF16)
    emat = (jnp.arange(S)[None, :] // NSA_SEL_LEN == jnp.arange(n_sel)[:, None]).astype(BF16)
    o_cmp, sel = cmp_attention(t["nq"], kc, vc, t["ng"], c2s, B, S, n_cmp, min(NSA_SEL_TOP, n_sel))
    o_slc = nsa_attention(t["nq"], t["nks"], t["nvs"], t["ng"], B, S, mode="select", branch=1, sel=sel, emat=emat)
    o_win = nsa_attention(t["nq"], t["nkw"], t["nvw"], t["ng"], B, S, mode="window", branch=2, window=NSA_WINDOW)

    o_d = stickbreak_attention(t["sq"], t["sk"], t["sv"], B, S)

    z = merge([o_a, o_b, o_cmp, o_slc, o_win, o_d], mg, p["w_branch"].astype(BF16))
    x = matmul(z, p["w_out"].astype(BF16), res=x)

    h2 = rmsnorm(x, p["ffn_norm"], BF16)
    pq = matmul(h2, p["peer_w_q"].astype(BF16))
    i1, i2, g = peer_topk(pq, p["peer_sub_k1"].astype(BF16), p["peer_sub_k2"].astype(BF16))

    def slots(a):
        return a.reshape(PEER_SLOTS, T).T.reshape(T, 1, PEER_SLOTS)

    w = peer_gates(slots(i1), slots(i2), slots(g))
    return peer_dense(h2, p["peer_u"].astype(BF16).T, p["peer_v"].astype(BF16), w, x)


def kernel(x, attn_norm, w_in, fox_forget_bias, mla_q_norm, mla_kv_norm, mla_w_uq, mla_w_ukv, nsa_cmp_k_pos,
           nsa_cmp_k_w1, nsa_cmp_k_w2, nsa_cmp_v_pos, nsa_cmp_v_w1, nsa_cmp_v_w2, w_branch, w_out, ffn_norm,
           peer_w_q, peer_sub_k1, peer_sub_k2, peer_u, peer_v, final_norm):
    B, S, D = x.shape
    stacked = dict(attn_norm=attn_norm, w_in=w_in, fox_forget_bias=fox_forget_bias, mla_q_norm=mla_q_norm,
                   mla_kv_norm=mla_kv_norm, mla_w_uq=mla_w_uq, mla_w_ukv=mla_w_ukv, nsa_cmp_k_pos=nsa_cmp_k_pos,
                   nsa_cmp_k_w1=nsa_cmp_k_w1, nsa_cmp_k_w2=nsa_cmp_k_w2, nsa_cmp_v_pos=nsa_cmp_v_pos,
                   nsa_cmp_v_w1=nsa_cmp_v_w1, nsa_cmp_v_w2=nsa_cmp_v_w2, w_branch=w_branch, w_out=w_out,
                   ffn_norm=ffn_norm, peer_w_q=peer_w_q, peer_sub_k1=peer_sub_k1, peer_sub_k2=peer_sub_k2,
                   peer_u=peer_u, peer_v=peer_v)
    xt = x.reshape(B * S, D)
    for l in range(w_in.shape[0]):
        xt = _layer(xt, B, S, {k: v[l] for k, v in stacked.items()})
    return rmsnorm(xt, final_norm, F32).reshape(B, S, D)
```

```python
import functools
import math

import jax
import jax.numpy as jnp
from jax import lax
from jax.experimental import pallas as pl
from jax.experimental.pallas import tpu as pltpu

F32, BF16, I32 = jnp.float32, jnp.bfloat16, jnp.int32

D_MODEL = 2048
HEAD_DIM = 64
MIX_HEADS = 8
N_BRANCH = 4
BRANCH_WIDTH = MIX_HEADS * HEAD_DIM
ROPE_THETA = 500000.0
PARTIAL_ROT = HEAD_DIM // 4
RMS_EPS = 1e-6
NEG = -1e30

MLA_Q_LORA = D_MODEL // 4
MLA_KV_LORA = D_MODEL // 8
MLA_NOPE = HEAD_DIM
MLA_ROPE = HEAD_DIM // 2
MLA_V = HEAD_DIM

NSA_KV_HEADS = 2
NSA_REP = MIX_HEADS // NSA_KV_HEADS
NSA_KV_WIDTH = NSA_KV_HEADS * HEAD_DIM
NSA_CMP_LEN = 32
NSA_CMP_STRIDE = 16
NSA_CMP_HIDDEN = 256
NSA_SEL_LEN = 64
NSA_SEL_TOP = 16
NSA_WINDOW = 512
NSA_FORCE_BONUS = 1e3

PEER_HEADS = 8
PEER_KEYS = 128
PEER_EXPERTS = PEER_KEYS * PEER_KEYS
PEER_TOPK = 16
PEER_QDIM = 256
PEER_SLOTS = PEER_HEADS * PEER_TOPK

LANES = 128
SUBLANES = 8
VMEM_LIMIT = 56 * 1024 * 1024
PAIR = LANES // HEAD_DIM
QK_SCALE = HEAD_DIM ** -0.5

_IN_NAMES = ("fq", "fk", "fv", "ff", "mcq", "mckv", "mkr", "nq", "nkc", "nvc", "nks", "nvs", "nkw", "nvw",
             "ng", "sq", "sk", "sv", "mg")
_IN_WIDTHS = (BRANCH_WIDTH, BRANCH_WIDTH, BRANCH_WIDTH, MIX_HEADS, MLA_Q_LORA, MLA_KV_LORA, MLA_ROPE,
              BRANCH_WIDTH, NSA_KV_WIDTH, NSA_KV_WIDTH, NSA_KV_WIDTH, NSA_KV_WIDTH, NSA_KV_WIDTH, NSA_KV_WIDTH,
              3 * MIX_HEADS, BRANCH_WIDTH, BRANCH_WIDTH, BRANCH_WIDTH, N_BRANCH * D_MODEL)
_IN_SRC = {}
_off = 0
for _n, _w in zip(_IN_NAMES, _IN_WIDTHS):
    _IN_SRC[_n] = (_off, _w)
    _off += _w


def _pieces(name):
    off, w = _IN_SRC[name]
    if name == "mkr":
        half = MLA_ROPE // 2
        return [(off, half)] * MIX_HEADS + [(off + half, half)] * MIX_HEADS
    if name in ("nks", "nvs", "nkw", "nvw"):
        return [(off + g * HEAD_DIM, HEAD_DIM) for g in range(NSA_KV_HEADS) for _ in range(PAIR)]
    return [(off, w)]


_PLAIN_ORDER = ("fq", "fk", "fv", "sq", "sk", "sv")
_REST_ORDER = ("mcq", "nq", "mckv", "ff", "mkr", "nkc", "nvc", "nks", "nvs", "nkw", "nvw", "ng")
_SEG = {}
for _order in (_PLAIN_ORDER, _REST_ORDER):
    _off = 0
    for _n in _order:
        _w = -(-sum(w for _, w in _pieces(_n)) // LANES) * LANES
        _SEG[_n] = (_off, _w)
        _off += _w
PLAIN_WIDTH = sum(_SEG[n][1] for n in _PLAIN_ORDER)
REST_WIDTH = sum(_SEG[n][1] for n in _REST_ORDER)


def _params(sem, vmem=None):
    return pltpu.CompilerParams(dimension_semantics=sem, vmem_limit_bytes=vmem)


def _mm(a, b):
    return jnp.dot(a, b, preferred_element_type=F32)


def _mm_nt(a, b):
    return lax.dot_general(a, b, (((1,), (1,)), ((), ())), preferred_element_type=F32)


def _split3(x):
    hi = x.astype(BF16)
    r = x - hi.astype(F32)
    mid = r.astype(BF16)
    lo = (r - mid.astype(F32)).astype(BF16)
    return hi, mid, lo


def _gelu(x):
    return 0.5 * x * (1.0 + lax.erf(x * (1.0 / math.sqrt(2.0))))


def _log_sigmoid(x):
    return jnp.minimum(x, 0.0) - jnp.log1p(jnp.exp(-jnp.abs(x)))


def _sigmoid(x):
    return 1.0 / (1.0 + jnp.exp(-x))


def _lane_col(x, idx):
    lane = lax.broadcasted_iota(I32, x.shape, 1)
    return jnp.sum(jnp.where(lane == idx, x, 0.0), axis=1, keepdims=True)


def _first_head(shape):
    return (lax.broadcasted_iota(I32, shape, 1) & (LANES - 1)) < HEAD_DIM


def _split_pair(x):
    first = _first_head(x.shape)
    zero = jnp.zeros_like(x)
    return jnp.where(first, x, zero), jnp.where(first, zero, x)


def _rmsnorm_kernel(x_ref, g_ref, o_ref):
    x = x_ref[...]
    y = x * lax.rsqrt(jnp.mean(x * x, axis=-1, keepdims=True) + RMS_EPS)
    o_ref[...] = (y * g_ref[...]).astype(o_ref.dtype)


def rmsnorm(x, g, out_dtype, tm=256):
    T, D = x.shape
    return pl.pallas_call(
        _rmsnorm_kernel,
        grid=(T // tm,),
        in_specs=[pl.BlockSpec((tm, D), lambda i: (i, 0)), pl.BlockSpec((1, D), lambda i: (0, 0))],
        out_specs=pl.BlockSpec((tm, D), lambda i: (i, 0)),
        out_shape=jax.ShapeDtypeStruct((T, D), out_dtype),
        compiler_params=_params(("parallel",)),
        name="rmsnorm",
    )(x, g.reshape(1, D))


def _matmul_kernel(a_ref, b_ref, *rest):
    o_ref = rest[-1]
    acc = _mm(a_ref[...], b_ref[...])
    if len(rest) == 2:
        acc = acc + rest[0][...]
    o_ref[...] = acc.astype(o_ref.dtype)


def matmul(a, b, res=None, out_dtype=F32, tm=1024, tn=1024):
    M, K = a.shape
    N = b.shape[1]
    tm, tn = min(tm, M), min(tn, N)
    in_specs = [pl.BlockSpec((tm, K), lambda j, i: (i, 0)), pl.BlockSpec((K, tn), lambda j, i: (0, j))]
    args = [a, b]
    if res is not None:
        in_specs.append(pl.BlockSpec((tm, tn), lambda j, i: (i, j)))
        args.append(res)
    return pl.pallas_call(
        _matmul_kernel,
        grid=(N // tn, M // tm),
        in_specs=in_specs,
        out_specs=pl.BlockSpec((tm, tn), lambda j, i: (i, j)),
        out_shape=jax.ShapeDtypeStruct((M, N), out_dtype),
        compiler_params=_params(("parallel", "parallel"), VMEM_LIMIT),
        name="matmul",
    )(*args)


def _rope_roll(x, cos, sin, sh, period):
    w = x.shape[-1]
    lane = lax.broadcasted_iota(I32, x.shape, 1) & (period - 1)
    sa = jnp.where(lane < sh, -sin, 0.0)
    sb = jnp.where((lane >= sh) & (lane < 2 * sh), sin, 0.0)
    x_up = pltpu.roll(x, w - sh, 1)
    x_dn = pltpu.roll(x, sh, 1)
    return x * cos + x_up * sa + x_dn * sb


def _rope_halves(x, cos, sin):
    h = x.shape[-1] // 2
    x1, x2 = x[:, :h], x[:, h:]
    return jnp.concatenate([x1 * cos - x2 * sin, x1 * sin + x2 * cos], axis=1)


def _prep_kernel(ps_ref, fb_ref, gq_ref, gkv_ref, wuq_ref, wukv_ref, ncos_ref, nsin_ref, mcos_ref, msin_ref,
                 lf_o, qn_o, qr_o, kn_o, mv_o, kr_o, nq_o, nkc_o, nvc_o, nks_o, nvs_o, nkw_o, nvw_o, ng_o):
    def seg(name):
        off, w = _SEG[name]
        return ps_ref[:, off:off + w]

    lf_o[...] = _log_sigmoid(seg("ff") + fb_ref[...])
    cq = seg("mcq")
    hq = cq * lax.rsqrt(jnp.mean(cq * cq, axis=-1, keepdims=True) + RMS_EPS) * gq_ref[...]
    qa = _mm(hq.astype(BF16), wuq_ref[...]) * (MLA_NOPE + MLA_ROPE) ** -0.5
    mcos, msin = mcos_ref[...], msin_ref[...]
    qn_o[...] = qa[:, :BRANCH_WIDTH].astype(BF16)
    qr_o[...] = _rope_halves(qa[:, BRANCH_WIDTH:], mcos, msin).astype(BF16)
    ckv = seg("mckv")
    hkv = ckv * lax.rsqrt(jnp.mean(ckv * ckv, axis=-1, keepdims=True) + RMS_EPS) * gkv_ref[...]
    kva = _mm(hkv.astype(BF16), wukv_ref[...])
    kn_o[...] = kva[:, :BRANCH_WIDTH].astype(BF16)
    mv_o[...] = kva[:, BRANCH_WIDTH:].astype(BF16)
    kr_o[...] = _rope_halves(seg("mkr"), mcos, msin).astype(BF16)
    ncos, nsin = ncos_ref[...], nsin_ref[...]
    kw = _SEG["nks"][1]
    kcos, ksin = ncos[:, :kw], nsin[:, :kw]
    nq_o[...] = (_rope_roll(seg("nq"), ncos, nsin, PARTIAL_ROT // 2, HEAD_DIM) * QK_SCALE).astype(BF16)
    nks_o[...] = _rope_roll(seg("nks"), kcos, ksin, PARTIAL_ROT // 2, HEAD_DIM).astype(BF16)
    nkw_o[...] = _rope_roll(seg("nkw"), kcos, ksin, PARTIAL_ROT // 2, HEAD_DIM).astype(BF16)
    nkc_o[...] = seg("nkc").astype(BF16)
    nvc_o[...] = seg("nvc").astype(BF16)
    nvs_o[...] = seg("nvs").astype(BF16)
    nvw_o[...] = seg("nvw").astype(BF16)
    ng_o[...] = seg("ng")


def prep(ps, fb, gq, gkv, wuq, wukv, ncos, nsin, mcos, msin, S, tm=256):
    T = ps.shape[0]
    n_pos = S // tm

    def tok(w):
        return pl.BlockSpec((tm, w), lambda i: (i, 0))

    def const(shape):
        return pl.BlockSpec(shape, lambda i: (0, 0))

    def pos(w):
        return pl.BlockSpec((tm, w), lambda i: (i % n_pos, 0))

    outs = [("lf", 128, F32),
            ("qn", 512, BF16), ("qr", 256, BF16), ("kn", 512, BF16), ("mv", 512, BF16), ("kr", 256, BF16),
            ("nq", 512, BF16), ("nkc", 128, BF16), ("nvc", 128, BF16), ("nks", 256, BF16), ("nvs", 256, BF16),
            ("nkw", 256, BF16), ("nvw", 256, BF16), ("ng", 128, F32)]
    res = pl.pallas_call(
        _prep_kernel,
        grid=(T // tm,),
        in_specs=[tok(REST_WIDTH), const(fb.shape), const(gq.shape), const(gkv.shape), const(wuq.shape),
                  const(wukv.shape), pos(512), pos(512), pos(128), pos(128)],
        out_specs=[tok(w) for _, w, _ in outs],
        out_shape=[jax.ShapeDtypeStruct((T, w), dt) for _, w, dt in outs],
        compiler_params=_params(("parallel",), VMEM_LIMIT),
        name="prep",
    )(ps, fb, gq, gkv, wuq, wukv, ncos, nsin, mcos, msin)
    return {n: r for (n, _, _), r in zip(outs, res)}


def _cumsum_kernel(x_ref, o_ref, *, S, ch):
    r = lax.broadcasted_iota(I32, (ch, ch), 0)
    c = lax.broadcasted_iota(I32, (ch, ch), 1)
    tri = jnp.where(c <= r, 1.0, 0.0).astype(BF16)
    carry = jnp.zeros((1, x_ref.shape[-1]), F32)
    for i in range(S // ch):
        hi, mid, lo = _split3(x_ref[i * ch:(i + 1) * ch, :])
        y = _mm(tri, hi) + _mm(tri, mid) + _mm(tri, lo) + carry
        o_ref[i * ch:(i + 1) * ch, :] = y
        carry = y[ch - 1:ch, :]


def cumsum_tokens(x, S, ch=256):
    T, W = x.shape
    return pl.pallas_call(
        functools.partial(_cumsum_kernel, S=S, ch=ch),
        grid=(T // S,),
        in_specs=[pl.BlockSpec((S, W), lambda b: (b, 0))],
        out_specs=pl.BlockSpec((S, W), lambda b: (b, 0)),
        out_shape=jax.ShapeDtypeStruct((T, W), F32),
        compiler_params=_params(("parallel",)),
        name="fox_cumsum",
    )(x)


def _value_pair(v):
    first = _first_head(v.shape)
    one = jnp.ones_like(v)
    return jnp.where(first, v, one), jnp.where(first, one, v)


def _softmax_tile(s, mask, m, acc, v):
    if mask is not None:
        s = jnp.where(mask, s, NEG)
    m_new = jnp.maximum(m, jnp.max(s, axis=-1, keepdims=True))
    p = jnp.exp(s - m_new)
    if mask is not None:
        p = jnp.where(mask, p, 0.0)
    return m_new, jnp.exp(m - m_new) * acc + _mm(p.astype(BF16), v)


def _softmax_init(t, n):
    return tuple((jnp.full((t, 1), NEG, F32), jnp.zeros((t, LANES), F32)) for _ in range(n))


def _normalize_pair(acc_a, acc_b):
    first = _first_head(acc_a.shape)
    return jnp.where(first, acc_a / acc_a[:, HEAD_DIM:HEAD_DIM + 1], acc_b / acc_b[:, 0:1])


def _causal_attn_kernel(*refs, t, fox, mla):
    if mla:
        q_ref, qr_ref, k_ref, kr_ref, v_ref, o_ref = refs
    elif fox:
        q_ref, k_ref, v_ref, fcol_ref, frow_ref, o_ref = refs
    hp, qi = pl.program_id(1), pl.program_id(2)
    heads = (PAIR * hp, PAIR * hp + 1)
    qs = _split_pair(q_ref[...])
    if mla:
        qr = qr_ref[...]
        lane_head = lax.shift_right_logical(lax.broadcasted_iota(I32, qr.shape, 1) & (LANES - 1),
                                            int(math.log2(MLA_ROPE // 2)))
        qs = tuple(jnp.concatenate([q, jnp.where(lane_head == h, qr, jnp.zeros_like(qr))], axis=1)
                   for q, h in zip(qs, heads))
    if fox:
        fcol = fcol_ref[...]
        fq = tuple(_lane_col(fcol, h) for h in heads)
    qpos = qi * t + lax.broadcasted_iota(I32, (t, 1), 0)

    def tile(j, carry, diag):
        k0 = pl.multiple_of(j * t, t)
        k = k_ref[pl.ds(k0, t), :]
        if mla:
            k = jnp.concatenate([k, kr_ref[pl.ds(k0, t), :]], axis=1)
        vs = _value_pair(v_ref[pl.ds(k0, t), :])
        mask = (k0 + lax.broadcasted_iota(I32, (1, t), 1)) <= qpos if diag else None
        new = []
        for a in range(PAIR):
            s = _mm_nt(qs[a], k)
            if fox:
                s = s + fq[a] - frow_ref[0, pl.ds(heads[a], 1), pl.ds(k0, t)]
            new.append(_softmax_tile(s, mask, *carry[a], vs[a]))
        return tuple(new)

    carry = lax.fori_loop(0, qi, lambda j, c: tile(j, c, False), _softmax_init(t, PAIR))
    (_, acc_a), (_, acc_b) = tile(qi, carry, True)
    o_ref[...] = _normalize_pair(acc_a, acc_b).astype(o_ref.dtype)


def _pair_specs(t, S, nq):
    def qspec(w, col=None):
        return pl.BlockSpec((t, w), (lambda b, h, i: (b * nq + i, 0)) if col is None
                            else (lambda b, h, i: (b * nq + i, col + h)))

    def kspec(w, col=None):
        return pl.BlockSpec((S, w), (lambda b, h, i: (b, 0)) if col is None else (lambda b, h, i: (b, col + h)))

    return qspec, kspec


def causal_attention(q, k, v, B, S, *, cols=(0, 0, 0), qr=None, kr=None, fcol=None, frow=None, t=512):
    T = q.shape[0]
    nq = S // t
    mla, fox = qr is not None, fcol is not None
    qspec, kspec = _pair_specs(t, S, nq)
    if mla:
        in_specs = [qspec(LANES, cols[0]), qspec(qr.shape[1]), kspec(LANES, cols[1]), kspec(kr.shape[1]),
                    kspec(LANES, cols[2])]
        args = [q, qr, k, kr, v]
    else:
        in_specs = [qspec(LANES, cols[0]), kspec(LANES, cols[1]), kspec(LANES, cols[2]), qspec(LANES),
                    pl.BlockSpec((1,) + frow.shape[1:], lambda b, h, i: (b, 0, 0))]
        args = [q, k, v, fcol, frow]
    return pl.pallas_call(
        functools.partial(_causal_attn_kernel, t=t, fox=fox, mla=mla),
        grid=(B, BRANCH_WIDTH // LANES, nq),
        in_specs=in_specs,
        out_specs=qspec(LANES, 0),
        out_shape=jax.ShapeDtypeStruct((T, BRANCH_WIDTH), BF16),
        compiler_params=_params(("parallel", "parallel", "parallel"), VMEM_LIMIT),
        name="attn_mla" if mla else "attn_fox",
    )(*args)


def _nsa_gates(ng, g, branch):
    return [_sigmoid(_lane_col(ng, (g * NSA_REP + h) * 3 + branch)) for h in range(NSA_REP)]


def _nsa_queries(q):
    qs = []
    for blk in range(q.shape[1] // LANES):
        qs.extend(_split_pair(q[:, blk * LANES:(blk + 1) * LANES]))
    return qs


def _nsa_attn_kernel(*refs, t, mode, window, branch):
    if mode == "select":
        q_ref, k_ref, v_ref, ng_ref, sel_ref, emat_ref, o_ref = refs
        sel = sel_ref[0]
    else:
        q_ref, k_ref, v_ref, ng_ref, o_ref = refs
    g, qi = pl.program_id(1), pl.program_id(2)
    qs = _nsa_queries(q_ref[...])
    gates = _nsa_gates(ng_ref[...], g, branch)
    qpos = qi * t + lax.broadcasted_iota(I32, (t, 1), 0)

    def tile(j, carry, diag):
        k0 = pl.multiple_of(j * t, t)
        k = k_ref[pl.ds(k0, t), :]
        vs = _value_pair(v_ref[pl.ds(k0, t), :])
        kpos = k0 + lax.broadcasted_iota(I32, (1, t), 1)
        if mode == "select":
            mask = lax.dot_general(sel, emat_ref[:, pl.ds(k0, t)], (((0,), (0,)), ((), ())),
                                   preferred_element_type=F32) > 0.5
            if diag:
                mask = mask & (kpos <= qpos)
        else:
            mask = (kpos <= qpos) & (kpos > qpos - window)
        return tuple(_softmax_tile(_mm_nt(qs[h], k), mask, *carry[h], vs[h % PAIR]) for h in range(NSA_REP))

    init = _softmax_init(t, NSA_REP)
    if mode == "select":
        carry = lax.fori_loop(0, qi, lambda j, c: tile(j, c, False), init)
        carry = tile(qi, carry, True)
    else:
        carry = lax.fori_loop(jnp.maximum(qi - window // t, 0), qi + 1, lambda j, c: tile(j, c, True), init)
    first = _first_head((t, LANES))
    blocks = []
    for blk in range(NSA_REP // PAIR):
        (_, acc_a), (_, acc_b) = carry[PAIR * blk], carry[PAIR * blk + 1]
        blocks.append(jnp.where(first, acc_a / acc_a[:, HEAD_DIM:HEAD_DIM + 1] * gates[PAIR * blk],
                                acc_b / acc_b[:, 0:1] * gates[PAIR * blk + 1]))
    o_ref[...] = jnp.concatenate(blocks, axis=1)


def nsa_attention(q, k, v, ng, B, S, *, mode, branch, sel=None, emat=None, window=None, t=512):
    T, W = q.shape
    nq = S // t
    gw = NSA_REP * HEAD_DIM

    def qspec(w, col):
        return pl.BlockSpec((t, w), (lambda b, g, i: (b * nq + i, g)) if col else (lambda b, g, i: (b * nq + i, 0)))

    kspec = pl.BlockSpec((S, LANES), lambda b, g, i: (b, g))
    in_specs = [qspec(gw, True), kspec, kspec, qspec(LANES, False)]
    args = [q, k, v, ng]
    if mode == "select":
        in_specs += [pl.BlockSpec((1, sel.shape[1], t), lambda b, g, i: (g, 0, b * nq + i)),
                     pl.BlockSpec(emat.shape, lambda b, g, i: (0, 0))]
        args += [sel, emat]
    return pl.pallas_call(
        functools.partial(_nsa_attn_kernel, t=t, mode=mode, window=window, branch=branch),
        grid=(B, W // gw, nq),
        in_specs=in_specs,
        out_specs=qspec(gw, True),
        out_shape=jax.ShapeDtypeStruct((T, W), F32),
        compiler_params=_params(("parallel", "parallel", "parallel"), VMEM_LIMIT),
        name="nsa_" + mode,
    )(*args)


def _stickbreak_kernel(q_ref, k_ref, v_ref, o_ref, *, t):
    qi = pl.program_id(2)
    qs = _split_pair(q_ref[...])
    qpos = qi * t + lax.broadcasted_iota(I32, (t, 1), 0)
    hb = t // 2
    r = lax.broadcasted_iota(I32, (2 * hb, hb), 0) & (hb - 1)
    c = lax.broadcasted_iota(I32, (2 * hb, hb), 1)
    from_here = jnp.where(r >= c, 1.0, 0.0).astype(BF16)

    def suffix_sums(x):
        hi = x.astype(BF16)
        lo = (x - hi.astype(F32)).astype(BF16)
        return _mm(jnp.concatenate([hi, lo], axis=1), from_here)

    def tile(j, carry, diag):
        k0 = pl.multiple_of(j * t, t)
        k = k_ref[pl.ds(k0, t), :]
        v = v_ref[pl.ds(k0, t), :]
        mask = (k0 + lax.broadcasted_iota(I32, (1, t), 1)) < qpos if diag else None
        new = []
        for a in range(PAIR):
            later, acc = carry[a]
            z = _mm_nt(qs[a], k)
            sp = jnp.maximum(z, 0.0) + jnp.log(1.0 + jnp.exp(-jnp.abs(z)))
            if diag:
                sp = jnp.where(mask, sp, 0.0)
            back = suffix_sums(sp[:, hb:])
            cum = jnp.concatenate([suffix_sums(sp[:, :hb]) + back[:, 0:1], back], axis=1)
            w = jnp.exp(z - cum - later)
            if diag:
                w = jnp.where(mask, w, 0.0)
            new.append((later + cum[:, 0:1], acc + _mm(w.astype(BF16), v)))
        return tuple(new)

    init = tuple((jnp.zeros((t, 1), F32), jnp.zeros((t, LANES), F32)) for _ in range(PAIR))
    carry = tile(qi, init, True)
    (_, acc_a), (_, acc_b) = lax.fori_loop(0, qi, lambda jj, cr: tile(qi - 1 - jj, cr, False), carry)
    o_ref[...] = jnp.where(_first_head(acc_a.shape), acc_a, acc_b).astype(o_ref.dtype)


def stickbreak_attention(q, k, v, B, S, cols=(0, 0, 0), t=512):
    T = q.shape[0]
    nq = S // t
    qspec, kspec = _pair_specs(t, S, nq)
    return pl.pallas_call(
        functools.partial(_stickbreak_kernel, t=t),
        grid=(B, BRANCH_WIDTH // LANES, nq),
        in_specs=[qspec(LANES, cols[0]), kspec(LANES, cols[1]), kspec(LANES, cols[2])],
        out_specs=qspec(LANES, 0),
        out_shape=jax.ShapeDtypeStruct((T, BRANCH_WIDTH), BF16),
        compiler_params=_params(("parallel", "parallel", "parallel"), VMEM_LIMIT),
        name="stickbreak",
    )(q, k, v)


def _compress_kernel(u_ref, pos_ref, w1_ref, w2_ref, w2s_ref, cos_ref, sin_ref, o_ref, *, rotary):
    u = u_ref[0]
    half = u.shape[-1]
    n = u.shape[0]
    first = _mm(u, w1_ref[:half, :])
    second = _mm(u, w1_ref[half:, :])
    bias = _mm(pos_ref[...], w1_ref[...])[0:1, :]
    hid = _gelu(first + pltpu.roll(second, n - 1, 0) + bias).astype(BF16)
    out = _mm(hid, w2_ref[...])
    if rotary:
        out = out * cos_ref[...] + _mm(hid, w2s_ref[...]) * sin_ref[...]
    o_ref[0] = jnp.concatenate([out] * PAIR, axis=1).astype(o_ref.dtype)


def compress(u, pos8, w1, w2, w2s, cos, sin, rotary):
    N, n, half = u.shape

    def const(a):
        return pl.BlockSpec(a.shape, lambda i: (0,) * a.ndim)

    return pl.pallas_call(
        functools.partial(_compress_kernel, rotary=rotary),
        grid=(N,),
        in_specs=[pl.BlockSpec((1, n, half), lambda i: (i, 0, 0)), const(pos8), const(w1), const(w2), const(w2s),
                  const(cos), const(sin)],
        out_specs=pl.BlockSpec((1, n, LANES), lambda i: (i, 0, 0)),
        out_shape=jax.ShapeDtypeStruct((N, n, LANES), BF16),
        compiler_params=_params(("parallel",)),
        name="nsa_compress",
    )(u, pos8, w1, w2, w2s, cos, sin)


def _cmp_attn_kernel(q_ref, kc_ref, vc_ref, ng_ref, c2s_ref, o_ref, sel_ref, *, t, n_cmp, n_top):
    g, qi = pl.program_id(1), pl.program_id(2)
    kc, vc = kc_ref[0], vc_ref[0]
    nc = kc.shape[0]
    n_sel = c2s_ref.shape[0]
    qs = _nsa_queries(q_ref[...])
    gates = _nsa_gates(ng_ref[...], g, 0)
    qpos = qi * t + lax.broadcasted_iota(I32, (t, 1), 0)
    cidx = lax.broadcasted_iota(I32, (1, nc), 1)
    cmask = (cidx * NSA_CMP_STRIDE + (NSA_CMP_LEN - 1) <= qpos) & (cidx < n_cmp)
    outs, psum = [], None
    for h in range(NSA_REP):
        s = jnp.where(cmask, _mm_nt(qs[h], kc), NEG)
        e = jnp.where(cmask, jnp.exp(s - jnp.max(s, axis=-1, keepdims=True)), 0.0)
        den = jnp.sum(e, axis=-1, keepdims=True)
        p = e / jnp.where(den > 0.0, den, 1.0)
        outs.append(_mm(p.astype(BF16), vc) * gates[h])
        psum = p if psum is None else psum + p
    first = _first_head((t, LANES))
    o_ref[...] = jnp.concatenate([jnp.where(first, outs[PAIR * b], outs[PAIR * b + 1])
                                  for b in range(NSA_REP // PAIR)], axis=1)

    hi, mid, lo = _split3(psum)
    c2s_t = c2s_ref[...]
    imp = _mm_nt(c2s_t, hi) + _mm_nt(c2s_t, mid) + _mm_nt(c2s_t, lo)
    qrow = qi * t + lax.broadcasted_iota(I32, (1, t), 1)
    blk = lax.broadcasted_iota(I32, (n_sel, 1), 0)
    cur = lax.shift_right_logical(qrow, int(math.log2(NSA_SEL_LEN)))
    forced = ((blk == 0) | (blk == cur) | (blk == cur - 1)).astype(F32)
    valid = blk * NSA_SEL_LEN <= qrow
    score = jnp.where(valid, imp + NSA_FORCE_BONUS * forced, NEG)
    rank = jnp.zeros((n_sel, t), F32)
    for i in range(n_sel):
        si = score[i:i + 1, :]
        ahead = (si > score) | ((si == score) & (blk > i))
        rank = rank + jnp.where(ahead, 1.0, 0.0)
    sel_ref[0] = jnp.where((rank < n_top) & valid, 1.0, 0.0).astype(sel_ref.dtype)


def cmp_attention(q, kc, vc, ng, c2s, B, S, n_cmp, n_top, t=128):
    T, W = q.shape
    nq = S // t
    gw = NSA_REP * HEAD_DIM
    G = W // gw
    n_sel = c2s.shape[0]

    def qspec(w, col):
        return pl.BlockSpec((t, w), (lambda b, g, i: (b * nq + i, g)) if col else (lambda b, g, i: (b * nq + i, 0)))

    cspec = pl.BlockSpec((1,) + kc.shape[1:], lambda b, g, i: (b * G + g, 0, 0))
    return pl.pallas_call(
        functools.partial(_cmp_attn_kernel, t=t, n_cmp=n_cmp, n_top=n_top),
        grid=(B, G, nq),
        in_specs=[qspec(gw, True), cspec, cspec, qspec(LANES, False), pl.BlockSpec(c2s.shape, lambda b, g, i: (0, 0))],
        out_specs=[qspec(gw, True), pl.BlockSpec((1, n_sel, t), lambda b, g, i: (g, 0, b * nq + i))],
        out_shape=[jax.ShapeDtypeStruct((T, W), F32), jax.ShapeDtypeStruct((G, n_sel, T), BF16)],
        compiler_params=_params(("parallel", "parallel", "parallel")),
        name="nsa_cmp_attn",
    )(q, kc, vc, ng, c2s)


def _merge_kernel(oa_ref, ob_ref, oc1_ref, oc2_ref, oc3_ref, od_ref, g0_ref, g1_ref, g2_ref, g3_ref, wb_ref, z_ref):
    oc = (oc1_ref[...] + oc2_ref[...] + oc3_ref[...]).astype(BF16)
    branches = (oa_ref[...], ob_ref[...], oc, od_ref[...])
    gates = (g0_ref, g1_ref, g2_ref, g3_ref)
    z = None
    for n in range(N_BRANCH):
        y = _sigmoid(gates[n][...]) * _mm(branches[n], wb_ref[n])
        z = y if z is None else z + y
    z_ref[...] = z.astype(z_ref.dtype)


def merge(outs, mg, wb, tm=512, tn=512):
    T = mg.shape[0]
    D = wb.shape[-1]
    nj = D // tn
    ospec = pl.BlockSpec((tm, BRANCH_WIDTH), lambda i, j: (i, 0))
    gspecs = [pl.BlockSpec((tm, tn), functools.partial(lambda i, j, n: (i, n * nj + j), n=n)) for n in range(N_BRANCH)]
    return pl.pallas_call(
        _merge_kernel,
        grid=(T // tm, nj),
        in_specs=[ospec] * 6 + gspecs + [pl.BlockSpec((N_BRANCH, BRANCH_WIDTH, tn), lambda i, j: (0, 0, j))],
        out_specs=pl.BlockSpec((tm, tn), lambda i, j: (i, j)),
        out_shape=jax.ShapeDtypeStruct((T, D), BF16),
        compiler_params=_params(("parallel", "parallel"), VMEM_LIMIT),
        name="merge",
    )(*outs, mg, mg, mg, mg, wb)


def _topk_rows(s, k, payload=None):
    n = s.shape[0]
    row = lax.broadcasted_iota(I32, s.shape, 0).astype(F32)
    vals, rows, pays = [], [], []
    for _ in range(k):
        m = jnp.max(s, axis=0, keepdims=True)
        am = jnp.min(jnp.where(s == m, row, float(n)), axis=0, keepdims=True)
        hit = row == am
        vals.append(m)
        rows.append(am)
        if payload is not None:
            pays.append(jnp.sum(jnp.where(hit, payload, 0.0), axis=0, keepdims=True))
        s = jnp.where(hit, -jnp.inf, s)
    vals, rows = jnp.concatenate(vals, axis=0), jnp.concatenate(rows, axis=0)
    return (vals, rows) if payload is None else (vals, rows, jnp.concatenate(pays, axis=0))


def _peer_topk_kernel(q_ref, k1_ref, k2_ref, i1_ref, i2_ref, g_ref):
    q = q_ref[...].astype(BF16)
    half = PEER_QDIM // 2
    v1, a1 = _topk_rows(_mm_nt(k1_ref[0], q[:, :half]), PEER_TOPK)
    v2, a2 = _topk_rows(_mm_nt(k2_ref[0], q[:, half:]), PEER_TOPK)
    h8 = PEER_TOPK // 2

    def pairs(x1, x2):
        rows = [x1[0:1] + x2]
        rows += [x1[a:a + 1] + x2[0:h8] for a in range(1, h8)]
        rows.append(x1[h8:] + x2[0:1])
        return jnp.concatenate(rows, axis=0)

    top_s, _, eidx = _topk_rows(pairs(v1, v2), PEER_TOPK, payload=pairs(a1 * float(PEER_KEYS), a2))
    ex = jnp.exp(top_s - top_s[0:1])
    g_ref[0] = ex / jnp.sum(ex, axis=0, keepdims=True)
    e = eidx.astype(I32)
    i1_ref[0] = lax.shift_right_logical(e, int(math.log2(PEER_KEYS)))
    i2_ref[0] = e & (PEER_KEYS - 1)


def peer_topk(q, k1, k2, tm=256):
    T = q.shape[0]
    H = k1.shape[0]
    ospec = pl.BlockSpec((1, PEER_TOPK, tm), lambda i, h: (h, 0, i))
    kspec = pl.BlockSpec((1,) + k1.shape[1:], lambda i, h: (h, 0, 0))
    return pl.pallas_call(
        _peer_topk_kernel,
        grid=(T // tm, H),
        in_specs=[pl.BlockSpec((tm, PEER_QDIM), lambda i, h: (i, h)), kspec, kspec],
        out_specs=[ospec, ospec, ospec],
        out_shape=[jax.ShapeDtypeStruct((H, PEER_TOPK, T), I32), jax.ShapeDtypeStruct((H, PEER_TOPK, T), I32),
                   jax.ShapeDtypeStruct((H, PEER_TOPK, T), F32)],
        compiler_params=_params(("parallel", "parallel")),
        name="peer_topk",
    )(q, k1, k2)


def _peer_gate_kernel(i1_ref, i2_ref, g_ref, o_ref, *, tm, nr):
    i1, i2, g = i1_ref[...], i2_ref[...], g_ref[...]
    n = PEER_KEYS
    g_hi = g.astype(BF16).astype(F32)
    g_lo = g - g_hi
    row = lax.broadcasted_iota(I32, (tm, n, PEER_SLOTS), 1)
    first = i1 == row
    a = jnp.concatenate([jnp.where(first, g_hi, 0.0).astype(BF16), jnp.where(first, g_lo, 0.0).astype(BF16)], axis=-1)
    second = jnp.where(i2 == row, 1.0, 0.0).astype(BF16)
    b = jnp.concatenate([second, second], axis=-1)
    w = lax.dot_general(a, b, (((2,), (2,)), ((0,), (0,))), preferred_element_type=F32)
    for ch in range(n // nr):
        o_ref[ch] = w[:, ch * nr:(ch + 1) * nr, :].reshape(tm * nr, n)


def peer_gates(i1, i2, g, nr=SUBLANES, tm=32):
    T = i1.shape[0]
    n = PEER_KEYS
    spec = pl.BlockSpec((tm, 1, PEER_SLOTS), lambda i: (i, 0, 0))
    return pl.pallas_call(
        functools.partial(_peer_gate_kernel, tm=tm, nr=nr),
        grid=(T // tm,),
        in_specs=[spec, spec, spec],
        out_specs=pl.BlockSpec((n // nr, tm * nr, n), lambda i: (0, i, 0)),
        out_shape=jax.ShapeDtypeStruct((n // nr, T * nr, n), F32),
        compiler_params=_params(("parallel",), VMEM_LIMIT),
        name="peer_gates",
    )(i1, i2, g)


def _peer_dense_kernel(h_ref, ut_ref, v_ref, w_ref, res_ref, o_ref, *, tm, nr):
    @pl.when(pl.program_id(1) == 0)
    def _():
        o_ref[...] = res_ref[...]

    act = _gelu(_mm(h_ref[...], ut_ref[...]))
    n = PEER_KEYS
    cols = []
    for r in range(nr):
        w_r = w_ref[pl.ds(r, tm, stride=nr), :]
        cols.append((w_r * act[:, r * n:(r + 1) * n]).astype(BF16))
    o_ref[...] += _mm(jnp.concatenate(cols, axis=1), v_ref[...])


def peer_dense(h, ut, v, w, res, nr=SUBLANES, tm=512):
    T, D = h.shape
    E = v.shape[0]
    te = nr * PEER_KEYS
    return pl.pallas_call(
        functools.partial(_peer_dense_kernel, tm=tm, nr=nr),
        grid=(T // tm, E // te),
        in_specs=[pl.BlockSpec((tm, D), lambda i, e: (i, 0)),
                  pl.BlockSpec((D, te), lambda i, e: (0, e)),
                  pl.BlockSpec((te, D), lambda i, e: (e, 0)),
                  pl.BlockSpec((None, tm * nr, PEER_KEYS), lambda i, e: (e, i, 0)),
                  pl.BlockSpec((tm, D), lambda i, e: (i, 0))],
        out_specs=pl.BlockSpec((tm, D), lambda i, e: (i, 0)),
        out_shape=jax.ShapeDtypeStruct((T, D), F32),
        compiler_params=_params(("parallel", "arbitrary"), VMEM_LIMIT),
        name="peer_dense",
    )(h, ut, v, w, res)


def _rope_tables(pos, half, reps, width):
    inv = jnp.exp(-math.log(ROPE_THETA) * jnp.arange(half, dtype=F32) / half)
    ang = pos.astype(F32)[:, None] * inv[None, :]
    pad = width - 2 * half
    cos = jnp.concatenate([jnp.cos(ang), jnp.cos(ang), jnp.ones((pos.shape[0], pad), F32)], axis=-1)
    sin = jnp.concatenate([jnp.sin(ang), jnp.sin(ang), jnp.zeros((pos.shape[0], pad), F32)], axis=-1)
    return jnp.tile(cos, (1, reps)), jnp.tile(sin, (1, reps))


def _projection_weight(w_in_l, order):
    cols, used = [], 0
    for name in order:
        scale = QK_SCALE if name in ("fq", "sq") else None
        for off, w in _pieces(name):
            piece = w_in_l[:, off:off + w]
            cols.append(piece if scale is None else piece * scale)
            used += w
        end = _SEG[name][0] + _SEG[name][1]
        if end > used:
            cols.append(jnp.zeros((w_in_l.shape[0], end - used), w_in_l.dtype))
            used = end
    return jnp.concatenate(cols, axis=1).astype(BF16)


def _layer(x, B, S, p):
    T = B * S
    H, G = MIX_HEADS, NSA_KV_HEADS
    pos = jnp.arange(S)

    h = rmsnorm(x, p["attn_norm"], BF16)
    plain = matmul(h, _projection_weight(p["w_in"], _PLAIN_ORDER), out_dtype=BF16)
    ps = matmul(h, _projection_weight(p["w_in"], _REST_ORDER))
    mg_off, mg_w = _IN_SRC["mg"]
    mg = matmul(h, p["w_in"][:, mg_off:mg_off + mg_w].astype(BF16))

    wuq = p["mla_w_uq"].reshape(MLA_Q_LORA, H, MLA_NOPE + MLA_ROPE)
    r_half = MLA_ROPE // 2
    wuq = jnp.concatenate([wuq[:, :, :MLA_NOPE].reshape(MLA_Q_LORA, -1),
                           wuq[:, :, MLA_NOPE:MLA_NOPE + r_half].reshape(MLA_Q_LORA, -1),
                           wuq[:, :, MLA_NOPE + r_half:].reshape(MLA_Q_LORA, -1)], axis=1).astype(BF16)
    wukv = p["mla_w_ukv"].reshape(MLA_KV_LORA, H, MLA_NOPE + MLA_V)
    wukv = jnp.concatenate([wukv[:, :, :MLA_NOPE].reshape(MLA_KV_LORA, -1),
                            wukv[:, :, MLA_NOPE:].reshape(MLA_KV_LORA, -1)], axis=1).astype(BF16)
    ncos, nsin = _rope_tables(pos, PARTIAL_ROT // 2, H, HEAD_DIM)
    mcos, msin = _rope_tables(pos, r_half, LANES // MLA_ROPE, MLA_ROPE)
    fb = jnp.pad(p["fox_forget_bias"], (0, LANES - H)).reshape(1, LANES)
    t = prep(ps, fb, p["mla_q_norm"].reshape(1, -1), p["mla_kv_norm"].reshape(1, -1), wuq, wukv,
             ncos, nsin, mcos, msin, S)

    fcum = cumsum_tokens(t["lf"], S)
    frow = fcum[:, :H].reshape(B, S, H).transpose(0, 2, 1)
    def cols(*names):
        return tuple(_SEG[n][0] // LANES for n in names)

    o_a = causal_attention(plain, plain, plain, B, S, cols=cols("fq", "fk", "fv"), fcol=fcum, frow=frow)

    o_b = causal_attention(t["qn"], t["kn"], t["mv"], B, S, qr=t["qr"], kr=t["kr"])

    n_half = S // NSA_CMP_STRIDE
    n_cmp = (S - NSA_CMP_LEN) // NSA_CMP_STRIDE + 1
    n_sel = S // NSA_SEL_LEN
    cmp_end = jnp.arange(n_half) * NSA_CMP_STRIDE + NSA_CMP_LEN - 1
    ccos, csin = _rope_tables(cmp_end, PARTIAL_ROT // 2, 1, HEAD_DIM)
    d = jnp.arange(HEAD_DIM)
    csin = jnp.where(d < PARTIAL_ROT // 2, -csin, csin)
    swap = jnp.where(d < PARTIAL_ROT // 2, d + PARTIAL_ROT // 2, d - PARTIAL_ROT // 2)

    def compressed(tok, pos_emb, w1, w2, rotary):
        u = tok.reshape(B, S, G, HEAD_DIM).transpose(0, 2, 1, 3).reshape(B * G, n_half, NSA_CMP_STRIDE * HEAD_DIM)
        pos8 = jnp.pad(pos_emb.reshape(1, -1), ((0, SUBLANES - 1), (0, 0))).astype(BF16)
        w2s = jnp.where(d[None, :] < PARTIAL_ROT, w2[:, swap], 0.0)
        return compress(u, pos8, w1.astype(BF16), w2.astype(BF16), w2s.astype(BF16), ccos, csin, rotary)

    kc = compressed(t["nkc"], p["nsa_cmp_k_pos"], p["nsa_cmp_k_w1"], p["nsa_cmp_k_w2"], True)
    vc = compressed(t["nvc"], p["nsa_cmp_v_pos"], p["nsa_cmp_v_w1"], p["nsa_cmp_v_w2"], False)
    cmp_start = jnp.arange(n_half) * NSA_CMP_STRIDE
    sel_lo = jnp.arange(n_sel) * NSA_SEL_LEN
    cover = jnp.clip(jnp.minimum(cmp_start[:, None] + NSA_CMP_LEN, sel_lo[None, :] + NSA_SEL_LEN)
                     - jnp.maximum(cmp_start[:, None], sel_lo[None, :]), 0, None)
    c2s = (cover.astype(F32) / NSA_CMP_LEN).astype(BF16).T
    emat = (jnp.arange(S)[None, :] // NSA_SEL_LEN == jnp.arange(n_sel)[:, None]).astype(BF16)
    o_cmp, sel = cmp_attention(t["nq"], kc, vc, t["ng"], c2s, B, S, n_cmp, min(NSA_SEL_TOP, n_sel))
    o_slc = nsa_attention(t["nq"], t["nks"], t["nvs"], t["ng"], B, S, mode="select", branch=1, sel=sel, emat=emat)
    o_win = nsa_attention(t["nq"], t["nkw"], t["nvw"], t["ng"], B, S, mode="window", branch=2, window=NSA_WINDOW)

    o_d = stickbreak_attention(plain, plain, plain, B, S, cols=cols("sq", "sk", "sv"))

    z = merge([o_a, o_b, o_cmp, o_slc, o_win, o_d], mg, p["w_branch"].astype(BF16))
    x = matmul(z, p["w_out"].astype(BF16), res=x)

    h2 = rmsnorm(x, p["ffn_norm"], BF16)
    pq = matmul(h2, p["peer_w_q"].astype(BF16))
    i1, i2, g = peer_topk(pq, p["peer_sub_k1"].astype(BF16), p["peer_sub_k2"].astype(BF16))

    def slots(a):
        return a.reshape(PEER_SLOTS, T).T.reshape(T, 1, PEER_SLOTS)

    w = peer_gates(slots(i1), slots(i2), slots(g))
    return peer_dense(h2, p["peer_u"].astype(BF16).T, p["peer_v"].astype(BF16), w, x)


def kernel(x, attn_norm, w_in, fox_forget_bias, mla_q_norm, mla_kv_norm, mla_w_uq, mla_w_ukv, nsa_cmp_k_pos,
           nsa_cmp_k_w1, nsa_cmp_k_w2, nsa_cmp_v_pos, nsa_cmp_v_w1, nsa_cmp_v_w2, w_branch, w_out, ffn_norm,
           peer_w_q, peer_sub_k1, peer_sub_k2, peer_u, peer_v, final_norm):
    B, S, D = x.shape
    stacked = dict(attn_norm=attn_norm, w_in=w_in, fox_forget_bias=fox_forget_bias, mla_q_norm=mla_q_norm,
                   mla_kv_norm=mla_kv_norm, mla_w_uq=mla_w_uq, mla_w_ukv=mla_w_ukv, nsa_cmp_k_pos=nsa_cmp_k_pos,
                   nsa_cmp_k_w1=nsa_cmp_k_w1, nsa_cmp_k_w2=nsa_cmp_k_w2, nsa_cmp_v_pos=nsa_cmp_v_pos,
                   nsa_cmp_v_w1=nsa_cmp_v_w1, nsa_cmp_v_w2=nsa_cmp_v_w2, w_branch=w_branch, w_out=w_out,
                   ffn_norm=ffn_norm, peer_w_q=peer_w_q, peer_sub_k1=peer_sub_k1, peer_sub_k2=peer_sub_k2,
                   peer_u=peer_u, peer_v=peer_v)
    xt = x.reshape(B * S, D)
    for l in range(w_in.shape[0]):
        xt = _layer(xt, B, S, {k: v[l] for k, v in stacked.items()})
    return rmsnorm(xt, final_norm, F32).reshape(B, S, D)
```

```python
import functools
import math

import jax
import jax.numpy as jnp
from jax import lax
from jax.experimental import pallas as pl
from jax.experimental.pallas import tpu as pltpu

F32, BF16, I32 = jnp.float32, jnp.bfloat16, jnp.int32

D_MODEL = 2048
HEAD_DIM = 64
MIX_HEADS = 8
N_BRANCH = 4
BRANCH_WIDTH = MIX_HEADS * HEAD_DIM
ROPE_THETA = 500000.0
PARTIAL_ROT = HEAD_DIM // 4
RMS_EPS = 1e-6
NEG = -1e30

MLA_Q_LORA = D_MODEL // 4
MLA_KV_LORA = D_MODEL // 8
MLA_NOPE = HEAD_DIM
MLA_ROPE = HEAD_DIM // 2
MLA_V = HEAD_DIM

NSA_KV_HEADS = 2
NSA_REP = MIX_HEADS // NSA_KV_HEADS
NSA_KV_WIDTH = NSA_KV_HEADS * HEAD_DIM
NSA_CMP_LEN = 32
NSA_CMP_STRIDE = 16
NSA_CMP_HIDDEN = 256
NSA_SEL_LEN = 64
NSA_SEL_TOP = 16
NSA_WINDOW = 512
NSA_FORCE_BONUS = 1e3

PEER_HEADS = 8
PEER_KEYS = 128
PEER_EXPERTS = PEER_KEYS * PEER_KEYS
PEER_TOPK = 16
PEER_QDIM = 256
PEER_SLOTS = PEER_HEADS * PEER_TOPK

LANES = 128
SUBLANES = 8
VMEM_LIMIT = 56 * 1024 * 1024
PAIR = LANES // HEAD_DIM
QK_SCALE = HEAD_DIM ** -0.5

_IN_NAMES = ("fq", "fk", "fv", "ff", "mcq", "mckv", "mkr", "nq", "nkc", "nvc", "nks", "nvs", "nkw", "nvw",
             "ng", "sq", "sk", "sv", "mg")
_IN_WIDTHS = (BRANCH_WIDTH, BRANCH_WIDTH, BRANCH_WIDTH, MIX_HEADS, MLA_Q_LORA, MLA_KV_LORA, MLA_ROPE,
              BRANCH_WIDTH, NSA_KV_WIDTH, NSA_KV_WIDTH, NSA_KV_WIDTH, NSA_KV_WIDTH, NSA_KV_WIDTH, NSA_KV_WIDTH,
              3 * MIX_HEADS, BRANCH_WIDTH, BRANCH_WIDTH, BRANCH_WIDTH, N_BRANCH * D_MODEL)
_IN_SRC = {}
_off = 0
for _n, _w in zip(_IN_NAMES, _IN_WIDTHS):
    _IN_SRC[_n] = (_off, _w)
    _off += _w


def _pieces(name):
    off, w = _IN_SRC[name]
    if name == "mkr":
        half = MLA_ROPE // 2
        return [(off, half)] * MIX_HEADS + [(off + half, half)] * MIX_HEADS
    if name in ("nks", "nvs", "nkw", "nvw"):
        return [(off + g * HEAD_DIM, HEAD_DIM) for g in range(NSA_KV_HEADS) for _ in range(PAIR)]
    return [(off, w)]


_PLAIN_ORDER = ("fq", "fk", "fv", "sq", "sk", "sv")
_REST_ORDER = ("mcq", "nq", "mckv", "ff", "mkr", "nkc", "nvc", "nks", "nvs", "nkw", "nvw", "ng")
_SEG = {}
for _order in (_PLAIN_ORDER, _REST_ORDER):
    _off = 0
    for _n in _order:
        _w = -(-sum(w for _, w in _pieces(_n)) // LANES) * LANES
        _SEG[_n] = (_off, _w)
        _off += _w
PLAIN_WIDTH = sum(_SEG[n][1] for n in _PLAIN_ORDER)
REST_WIDTH = sum(_SEG[n][1] for n in _REST_ORDER)


def _params(sem, vmem=None):
    return pltpu.CompilerParams(dimension_semantics=sem, vmem_limit_bytes=vmem)


def _mm(a, b):
    return jnp.dot(a, b, preferred_element_type=F32)


def _mm_nt(a, b):
    return lax.dot_general(a, b, (((1,), (1,)), ((), ())), preferred_element_type=F32)


def _split3(x):
    hi = x.astype(BF16)
    r = x - hi.astype(F32)
    mid = r.astype(BF16)
    lo = (r - mid.astype(F32)).astype(BF16)
    return hi, mid, lo


def _gelu(x):
    return 0.5 * x * (1.0 + lax.erf(x * (1.0 / math.sqrt(2.0))))


def _log_sigmoid(x):
    return jnp.minimum(x, 0.0) - jnp.log1p(jnp.exp(-jnp.abs(x)))


def _sigmoid(x):
    return 1.0 / (1.0 + jnp.exp(-x))


def _lane_col(x, idx):
    lane = lax.broadcasted_iota(I32, x.shape, 1)
    return jnp.sum(jnp.where(lane == idx, x, 0.0), axis=1, keepdims=True)


def _first_head(shape):
    return (lax.broadcasted_iota(I32, shape, 1) & (LANES - 1)) < HEAD_DIM


def _split_pair(x):
    first = _first_head(x.shape)
    zero = jnp.zeros_like(x)
    return jnp.where(first, x, zero), jnp.where(first, zero, x)


def _rms(x, g):
    return x * lax.rsqrt(jnp.mean(x * x, axis=-1, keepdims=True) + RMS_EPS) * g


def _rmsnorm_kernel(x_ref, g_ref, o_ref):
    o_ref[...] = _rms(x_ref[...], g_ref[...]).astype(o_ref.dtype)


def rmsnorm(x, g, out_dtype, tm=256):
    T, D = x.shape
    return pl.pallas_call(
        _rmsnorm_kernel,
        grid=(T // tm,),
        in_specs=[pl.BlockSpec((tm, D), lambda i: (i, 0)), pl.BlockSpec((1, D), lambda i: (0, 0))],
        out_specs=pl.BlockSpec((tm, D), lambda i: (i, 0)),
        out_shape=jax.ShapeDtypeStruct((T, D), out_dtype),
        compiler_params=_params(("parallel",)),
        name="rmsnorm",
    )(x, g.reshape(1, D))


def _matmul_kernel(a_ref, b_ref, o_ref):
    o_ref[...] = _mm(a_ref[...], b_ref[...]).astype(o_ref.dtype)


def matmul(a, b, out_dtype=F32, cols=None, tm=1024, tn=1024):
    M, K = a.shape
    c0, N = (0, b.shape[1]) if cols is None else cols
    tm, tn = min(tm, M), min(tn, N)
    assert c0 % tn == 0 and N % tn == 0 and M % tm == 0
    j0 = c0 // tn
    return pl.pallas_call(
        _matmul_kernel,
        grid=(N // tn, M // tm),
        in_specs=[pl.BlockSpec((tm, K), lambda j, i: (i, 0)), pl.BlockSpec((K, tn), lambda j, i: (0, j0 + j))],
        out_specs=pl.BlockSpec((tm, tn), lambda j, i: (i, j)),
        out_shape=jax.ShapeDtypeStruct((M, N), out_dtype),
        compiler_params=_params(("parallel", "parallel"), VMEM_LIMIT),
        name="matmul",
    )(a, b)


def _matmul_norm_kernel(a_ref, b_ref, res_ref, g_ref, o_ref, hn_ref):
    x = _mm(a_ref[...], b_ref[...]) + res_ref[...]
    o_ref[...] = x
    hn_ref[...] = _rms(x, g_ref[...]).astype(hn_ref.dtype)


def matmul_norm(a, b, res, g, tm=512):
    M, K = a.shape
    N = b.shape[1]
    row = pl.BlockSpec((tm, N), lambda i: (i, 0))
    return pl.pallas_call(
        _matmul_norm_kernel,
        grid=(M // tm,),
        in_specs=[pl.BlockSpec((tm, K), lambda i: (i, 0)), pl.BlockSpec((K, N), lambda i: (0, 0)), row,
                  pl.BlockSpec((1, N), lambda i: (0, 0))],
        out_specs=[row, row],
        out_shape=[jax.ShapeDtypeStruct((M, N), F32), jax.ShapeDtypeStruct((M, N), BF16)],
        compiler_params=_params(("parallel",), VMEM_LIMIT),
        name="matmul_norm",
    )(a, b, res, g.reshape(1, N))


def _rope_roll(x, cos, sin, sh, period):
    w = x.shape[-1]
    lane = lax.broadcasted_iota(I32, x.shape, 1) & (period - 1)
    sa = jnp.where(lane < sh, -sin, 0.0)
    sb = jnp.where((lane >= sh) & (lane < 2 * sh), sin, 0.0)
    x_up = pltpu.roll(x, w - sh, 1)
    x_dn = pltpu.roll(x, sh, 1)
    return x * cos + x_up * sa + x_dn * sb


def _rope_halves(x, cos, sin):
    h = x.shape[-1] // 2
    x1, x2 = x[:, :h], x[:, h:]
    return jnp.concatenate([x1 * cos - x2 * sin, x1 * sin + x2 * cos], axis=1)


def _prep_kernel(ps_ref, fb_ref, gq_ref, gkv_ref, wuq_ref, wukv_ref, ncos_ref, nsin_ref, mcos_ref, msin_ref,
                 lf_o, qn_o, qr_o, kn_o, mv_o, kr_o, nq_o, nkc_o, nvc_o, nks_o, nvs_o, nkw_o, nvw_o, ng_o):
    def seg(name):
        off, w = _SEG[name]
        return ps_ref[:, off:off + w]

    lf_o[...] = _log_sigmoid(seg("ff") + fb_ref[...])
    cq = seg("mcq")
    hq = cq * lax.rsqrt(jnp.mean(cq * cq, axis=-1, keepdims=True) + RMS_EPS) * gq_ref[...]
    qa = _mm(hq.astype(BF16), wuq_ref[...]) * (MLA_NOPE + MLA_ROPE) ** -0.5
    mcos, msin = mcos_ref[...], msin_ref[...]
    qn_o[...] = qa[:, :BRANCH_WIDTH].astype(BF16)
    qr_o[...] = _rope_halves(qa[:, BRANCH_WIDTH:], mcos, msin).astype(BF16)
    ckv = seg("mckv")
    hkv = ckv * lax.rsqrt(jnp.mean(ckv * ckv, axis=-1, keepdims=True) + RMS_EPS) * gkv_ref[...]
    kva = _mm(hkv.astype(BF16), wukv_ref[...])
    kn_o[...] = kva[:, :BRANCH_WIDTH].astype(BF16)
    mv_o[...] = kva[:, BRANCH_WIDTH:].astype(BF16)
    kr_o[...] = _rope_halves(seg("mkr"), mcos, msin).astype(BF16)
    ncos, nsin = ncos_ref[...], nsin_ref[...]
    kw = _SEG["nks"][1]
    kcos, ksin = ncos[:, :kw], nsin[:, :kw]
    nq_o[...] = (_rope_roll(seg("nq"), ncos, nsin, PARTIAL_ROT // 2, HEAD_DIM) * QK_SCALE).astype(BF16)
    nks_o[...] = _rope_roll(seg("nks"), kcos, ksin, PARTIAL_ROT // 2, HEAD_DIM).astype(BF16)
    nkw_o[...] = _rope_roll(seg("nkw"), kcos, ksin, PARTIAL_ROT // 2, HEAD_DIM).astype(BF16)
    nkc_o[...] = seg("nkc").astype(BF16)
    nvc_o[...] = seg("nvc").astype(BF16)
    nvs_o[...] = seg("nvs").astype(BF16)
    nvw_o[...] = seg("nvw").astype(BF16)
    ng_o[...] = seg("ng")


def prep(ps, fb, gq, gkv, wuq, wukv, ncos, nsin, mcos, msin, S, tm=256):
    T = ps.shape[0]
    n_pos = S // tm

    def tok(w):
        return pl.BlockSpec((tm, w), lambda i: (i, 0))

    def const(shape):
        return pl.BlockSpec(shape, lambda i: (0, 0))

    def pos(w):
        return pl.BlockSpec((tm, w), lambda i: (i % n_pos, 0))

    outs = [("lf", 128, F32),
            ("qn", 512, BF16), ("qr", 256, BF16), ("kn", 512, BF16), ("mv", 512, BF16), ("kr", 256, BF16),
            ("nq", 512, BF16), ("nkc", 128, BF16), ("nvc", 128, BF16), ("nks", 256, BF16), ("nvs", 256, BF16),
            ("nkw", 256, BF16), ("nvw", 256, BF16), ("ng", 128, F32)]
    res = pl.pallas_call(
        _prep_kernel,
        grid=(T // tm,),
        in_specs=[tok(REST_WIDTH), const(fb.shape), const(gq.shape), const(gkv.shape), const(wuq.shape),
                  const(wukv.shape), pos(512), pos(512), pos(128), pos(128)],
        out_specs=[tok(w) for _, w, _ in outs],
        out_shape=[jax.ShapeDtypeStruct((T, w), dt) for _, w, dt in outs],
        compiler_params=_params(("parallel",), VMEM_LIMIT),
        name="prep",
    )(ps, fb, gq, gkv, wuq, wukv, ncos, nsin, mcos, msin)
    return {n: r for (n, _, _), r in zip(outs, res)}


def _cumsum_kernel(x_ref, o_ref, *, S, ch):
    r = lax.broadcasted_iota(I32, (ch, ch), 0)
    c = lax.broadcasted_iota(I32, (ch, ch), 1)
    tri = jnp.where(c <= r, 1.0, 0.0).astype(BF16)
    carry = jnp.zeros((1, x_ref.shape[-1]), F32)
    for i in range(S // ch):
        hi, mid, lo = _split3(x_ref[i * ch:(i + 1) * ch, :])
        y = _mm(tri, hi) + _mm(tri, mid) + _mm(tri, lo) + carry
        o_ref[i * ch:(i + 1) * ch, :] = y
        carry = y[ch - 1:ch, :]


def cumsum_tokens(x, S, ch=256):
    T, W = x.shape
    return pl.pallas_call(
        functools.partial(_cumsum_kernel, S=S, ch=ch),
        grid=(T // S,),
        in_specs=[pl.BlockSpec((S, W), lambda b: (b, 0))],
        out_specs=pl.BlockSpec((S, W), lambda b: (b, 0)),
        out_shape=jax.ShapeDtypeStruct((T, W), F32),
        compiler_params=_params(("parallel",)),
        name="fox_cumsum",
    )(x)


def _value_pair(v):
    first = _first_head(v.shape)
    one = jnp.ones_like(v)
    return jnp.where(first, v, one), jnp.where(first, one, v)


def _softmax_tile(s, mask, m, acc, v):
    if mask is not None:
        s = jnp.where(mask, s, NEG)
    m_new = jnp.maximum(m, jnp.max(s, axis=-1, keepdims=True))
    p = jnp.exp(s - m_new)
    if mask is not None:
        p = jnp.where(mask, p, 0.0)
    return m_new, jnp.exp(m - m_new) * acc + _mm(p.astype(BF16), v)


def _softmax_init(t, n):
    return tuple((jnp.full((t, 1), NEG, F32), jnp.zeros((t, LANES), F32)) for _ in range(n))


def _normalize_pair(acc_a, acc_b):
    first = _first_head(acc_a.shape)
    return jnp.where(first, acc_a / acc_a[:, HEAD_DIM:HEAD_DIM + 1], acc_b / acc_b[:, 0:1])


def _causal_attn_kernel(*refs, t, fox, mla):
    if mla:
        q_ref, qr_ref, k_ref, kr_ref, v_ref, o_ref = refs
    elif fox:
        q_ref, k_ref, v_ref, fcol_ref, frow_ref, o_ref = refs
    hp, qi = pl.program_id(1), pl.program_id(2)
    heads = (PAIR * hp, PAIR * hp + 1)
    qs = _split_pair(q_ref[...])
    if mla:
        qr = qr_ref[...]
        lane_head = lax.shift_right_logical(lax.broadcasted_iota(I32, qr.shape, 1) & (LANES - 1),
                                            int(math.log2(MLA_ROPE // 2)))
        qs = tuple(jnp.concatenate([q, jnp.where(lane_head == h, qr, jnp.zeros_like(qr))], axis=1)
                   for q, h in zip(qs, heads))
    if fox:
        fcol = fcol_ref[...]
        fq = tuple(_lane_col(fcol, h) for h in heads)
    qpos = qi * t + lax.broadcasted_iota(I32, (t, 1), 0)

    def tile(j, carry, diag):
        k0 = pl.multiple_of(j * t, t)
        k = k_ref[pl.ds(k0, t), :]
        if mla:
            k = jnp.concatenate([k, kr_ref[pl.ds(k0, t), :]], axis=1)
        vs = _value_pair(v_ref[pl.ds(k0, t), :])
        mask = (k0 + lax.broadcasted_iota(I32, (1, t), 1)) <= qpos if diag else None
        new = []
        for a in range(PAIR):
            s = _mm_nt(qs[a], k)
            if fox:
                s = s + fq[a] - frow_ref[0, pl.ds(heads[a], 1), pl.ds(k0, t)]
            new.append(_softmax_tile(s, mask, *carry[a], vs[a]))
        return tuple(new)

    carry = lax.fori_loop(0, qi, lambda j, c: tile(j, c, False), _softmax_init(t, PAIR))
    (_, acc_a), (_, acc_b) = tile(qi, carry, True)
    o_ref[...] = _normalize_pair(acc_a, acc_b).astype(o_ref.dtype)


def _pair_specs(t, S, nq):
    def qspec(w, col=None):
        return pl.BlockSpec((t, w), (lambda b, h, i: (b * nq + i, 0)) if col is None
                            else (lambda b, h, i: (b * nq + i, col + h)))

    def kspec(w, col=None):
        return pl.BlockSpec((S, w), (lambda b, h, i: (b, 0)) if col is None else (lambda b, h, i: (b, col + h)))

    return qspec, kspec


def causal_attention(q, k, v, B, S, *, cols=(0, 0, 0), qr=None, kr=None, fcol=None, frow=None, t=512):
    T = q.shape[0]
    nq = S // t
    mla, fox = qr is not None, fcol is not None
    qspec, kspec = _pair_specs(t, S, nq)
    if mla:
        in_specs = [qspec(LANES, cols[0]), qspec(qr.shape[1]), kspec(LANES, cols[1]), kspec(kr.shape[1]),
                    kspec(LANES, cols[2])]
        args = [q, qr, k, kr, v]
    else:
        in_specs = [qspec(LANES, cols[0]), kspec(LANES, cols[1]), kspec(LANES, cols[2]), qspec(LANES),
                    pl.BlockSpec((1,) + frow.shape[1:], lambda b, h, i: (b, 0, 0))]
        args = [q, k, v, fcol, frow]
    return pl.pallas_call(
        functools.partial(_causal_attn_kernel, t=t, fox=fox, mla=mla),
        grid=(B, BRANCH_WIDTH // LANES, nq),
        in_specs=in_specs,
        out_specs=qspec(LANES, 0),
        out_shape=jax.ShapeDtypeStruct((T, BRANCH_WIDTH), BF16),
        compiler_params=_params(("parallel", "parallel", "parallel"), VMEM_LIMIT),
        name="attn_mla" if mla else "attn_fox",
    )(*args)


def _nsa_gates(ng, g, branch):
    return [_sigmoid(_lane_col(ng, (g * NSA_REP + h) * 3 + branch)) for h in range(NSA_REP)]


def _nsa_queries(q):
    qs = []
    for blk in range(q.shape[1] // LANES):
        qs.extend(_split_pair(q[:, blk * LANES:(blk + 1) * LANES]))
    return qs


def _nsa_attn_kernel(*refs, t, mode, window, branch):
    if mode == "select":
        q_ref, k_ref, v_ref, ng_ref, sel_ref, emat_ref, o_ref = refs
        sel = sel_ref[0]
    else:
        q_ref, k_ref, v_ref, ng_ref, o_ref = refs
    g, qi = pl.program_id(1), pl.program_id(2)
    qs = _nsa_queries(q_ref[...])
    gates = _nsa_gates(ng_ref[...], g, branch)
    qpos = qi * t + lax.broadcasted_iota(I32, (t, 1), 0)

    def tile(j, carry, diag):
        k0 = pl.multiple_of(j * t, t)
        k = k_ref[pl.ds(k0, t), :]
        vs = _value_pair(v_ref[pl.ds(k0, t), :])
        kpos = k0 + lax.broadcasted_iota(I32, (1, t), 1)
        if mode == "select":
            mask = lax.dot_general(sel, emat_ref[:, pl.ds(k0, t)], (((0,), (0,)), ((), ())),
                                   preferred_element_type=F32) > 0.5
            if diag:
                mask = mask & (kpos <= qpos)
        else:
            mask = (kpos <= qpos) & (kpos > qpos - window)
        return tuple(_softmax_tile(_mm_nt(qs[h], k), mask, *carry[h], vs[h % PAIR]) for h in range(NSA_REP))

    init = _softmax_init(t, NSA_REP)
    if mode == "select":
        carry = lax.fori_loop(0, qi, lambda j, c: tile(j, c, False), init)
        carry = tile(qi, carry, True)
    else:
        carry = lax.fori_loop(jnp.maximum(qi - window // t, 0), qi + 1, lambda j, c: tile(j, c, True), init)
    first = _first_head((t, LANES))
    blocks = []
    for blk in range(NSA_REP // PAIR):
        (_, acc_a), (_, acc_b) = carry[PAIR * blk], carry[PAIR * blk + 1]
        blocks.append(jnp.where(first, acc_a / acc_a[:, HEAD_DIM:HEAD_DIM + 1] * gates[PAIR * blk],
                                acc_b / acc_b[:, 0:1] * gates[PAIR * blk + 1]))
    o_ref[...] = jnp.concatenate(blocks, axis=1)


def nsa_attention(q, k, v, ng, B, S, *, mode, branch, sel=None, emat=None, window=None, t=512):
    T, W = q.shape
    nq = S // t
    gw = NSA_REP * HEAD_DIM

    def qspec(w, col):
        return pl.BlockSpec((t, w), (lambda b, g, i: (b * nq + i, g)) if col else (lambda b, g, i: (b * nq + i, 0)))

    kspec = pl.BlockSpec((S, LANES), lambda b, g, i: (b, g))
    in_specs = [qspec(gw, True), kspec, kspec, qspec(LANES, False)]
    args = [q, k, v, ng]
    if mode == "select":
        in_specs += [pl.BlockSpec((1, sel.shape[1], t), lambda b, g, i: (g, 0, b * nq + i)),
                     pl.BlockSpec(emat.shape, lambda b, g, i: (0, 0))]
        args += [sel, emat]
    return pl.pallas_call(
        functools.partial(_nsa_attn_kernel, t=t, mode=mode, window=window, branch=branch),
        grid=(B, W // gw, nq),
        in_specs=in_specs,
        out_specs=qspec(gw, True),
        out_shape=jax.ShapeDtypeStruct((T, W), F32),
        compiler_params=_params(("parallel", "parallel", "parallel"), VMEM_LIMIT),
        name="nsa_" + mode,
    )(*args)


def _stickbreak_kernel(q_ref, k_ref, v_ref, o_ref, *, t):
    qi = pl.program_id(2)
    qs = _split_pair(q_ref[...])
    qpos = qi * t + lax.broadcasted_iota(I32, (t, 1), 0)
    hb = t // 2
    r = lax.broadcasted_iota(I32, (2 * hb, hb), 0) & (hb - 1)
    c = lax.broadcasted_iota(I32, (2 * hb, hb), 1)
    from_here = jnp.where(r >= c, 1.0, 0.0).astype(BF16)

    def suffix_sums(x):
        hi = x.astype(BF16)
        lo = (x - hi.astype(F32)).astype(BF16)
        return _mm(jnp.concatenate([hi, lo], axis=1), from_here)

    def tile(j, carry, diag):
        k0 = pl.multiple_of(j * t, t)
        k = k_ref[pl.ds(k0, t), :]
        v = v_ref[pl.ds(k0, t), :]
        mask = (k0 + lax.broadcasted_iota(I32, (1, t), 1)) < qpos if diag else None
        new = []
        for a in range(PAIR):
            later, acc = carry[a]
            z = _mm_nt(qs[a], k)
            sp = jnp.maximum(z, 0.0) + jnp.log(1.0 + jnp.exp(-jnp.abs(z)))
            if diag:
                sp = jnp.where(mask, sp, 0.0)
            back = suffix_sums(sp[:, hb:])
            cum = jnp.concatenate([suffix_sums(sp[:, :hb]) + back[:, 0:1], back], axis=1)
            w = jnp.exp(z - cum - later)
            if diag:
                w = jnp.where(mask, w, 0.0)
            new.append((later + cum[:, 0:1], acc + _mm(w.astype(BF16), v)))
        return tuple(new)

    init = tuple((jnp.zeros((t, 1), F32), jnp.zeros((t, LANES), F32)) for _ in range(PAIR))
    carry = tile(qi, init, True)
    (_, acc_a), (_, acc_b) = lax.fori_loop(0, qi, lambda jj, cr: tile(qi - 1 - jj, cr, False), carry)
    o_ref[...] = jnp.where(_first_head(acc_a.shape), acc_a, acc_b).astype(o_ref.dtype)


def stickbreak_attention(q, k, v, B, S, cols=(0, 0, 0), t=512):
    T = q.shape[0]
    nq = S // t
    qspec, kspec = _pair_specs(t, S, nq)
    return pl.pallas_call(
        functools.partial(_stickbreak_kernel, t=t),
        grid=(B, BRANCH_WIDTH // LANES, nq),
        in_specs=[qspec(LANES, cols[0]), kspec(LANES, cols[1]), kspec(LANES, cols[2])],
        out_specs=qspec(LANES, 0),
        out_shape=jax.ShapeDtypeStruct((T, BRANCH_WIDTH), BF16),
        compiler_params=_params(("parallel", "parallel", "parallel"), VMEM_LIMIT),
        name="stickbreak",
    )(q, k, v)


def _compress_kernel(u_ref, pos_ref, w1_ref, w2_ref, w2s_ref, cos_ref, sin_ref, o_ref, *, rotary):
    u = u_ref[0]
    half = u.shape[-1]
    n = u.shape[0]
    first = _mm(u, w1_ref[:half, :])
    second = _mm(u, w1_ref[half:, :])
    bias = _mm(pos_ref[...], w1_ref[...])[0:1, :]
    hid = _gelu(first + pltpu.roll(second, n - 1, 0) + bias).astype(BF16)
    out = _mm(hid, w2_ref[...])
    if rotary:
        out = out * cos_ref[...] + _mm(hid, w2s_ref[...]) * sin_ref[...]
    o_ref[0] = jnp.concatenate([out] * PAIR, axis=1).astype(o_ref.dtype)


def compress(u, pos8, w1, w2, w2s, cos, sin, rotary):
    N, n, half = u.shape

    def const(a):
        return pl.BlockSpec(a.shape, lambda i: (0,) * a.ndim)

    return pl.pallas_call(
        functools.partial(_compress_kernel, rotary=rotary),
        grid=(N,),
        in_specs=[pl.BlockSpec((1, n, half), lambda i: (i, 0, 0)), const(pos8), const(w1), const(w2), const(w2s),
                  const(cos), const(sin)],
        out_specs=pl.BlockSpec((1, n, LANES), lambda i: (i, 0, 0)),
        out_shape=jax.ShapeDtypeStruct((N, n, LANES), BF16),
        compiler_params=_params(("parallel",)),
        name="nsa_compress",
    )(u, pos8, w1, w2, w2s, cos, sin)


def _cmp_attn_kernel(q_ref, kc_ref, vc_ref, ng_ref, c2s_ref, o_ref, sel_ref, *, t, n_cmp, n_top):
    g, qi = pl.program_id(1), pl.program_id(2)
    kc, vc = kc_ref[0], vc_ref[0]
    nc = kc.shape[0]
    n_sel = c2s_ref.shape[0]
    qs = _nsa_queries(q_ref[...])
    gates = _nsa_gates(ng_ref[...], g, 0)
    qpos = qi * t + lax.broadcasted_iota(I32, (t, 1), 0)
    cidx = lax.broadcasted_iota(I32, (1, nc), 1)
    cmask = (cidx * NSA_CMP_STRIDE + (NSA_CMP_LEN - 1) <= qpos) & (cidx < n_cmp)
    outs, psum = [], None
    for h in range(NSA_REP):
        s = jnp.where(cmask, _mm_nt(qs[h], kc), NEG)
        e = jnp.where(cmask, jnp.exp(s - jnp.max(s, axis=-1, keepdims=True)), 0.0)
        den = jnp.sum(e, axis=-1, keepdims=True)
        p = e / jnp.where(den > 0.0, den, 1.0)
        outs.append(_mm(p.astype(BF16), vc) * gates[h])
        psum = p if psum is None else psum + p
    first = _first_head((t, LANES))
    o_ref[...] = jnp.concatenate([jnp.where(first, outs[PAIR * b], outs[PAIR * b + 1])
                                  for b in range(NSA_REP // PAIR)], axis=1)

    hi, mid, lo = _split3(psum)
    c2s_t = c2s_ref[...]
    imp = _mm_nt(c2s_t, hi) + _mm_nt(c2s_t, mid) + _mm_nt(c2s_t, lo)
    qrow = qi * t + lax.broadcasted_iota(I32, (1, t), 1)
    blk = lax.broadcasted_iota(I32, (n_sel, 1), 0)
    cur = lax.shift_right_logical(qrow, int(math.log2(NSA_SEL_LEN)))
    forced = ((blk == 0) | (blk == cur) | (blk == cur - 1)).astype(F32)
    valid = blk * NSA_SEL_LEN <= qrow
    score = jnp.where(valid, imp + NSA_FORCE_BONUS * forced, NEG)
    rank = jnp.zeros((n_sel, t), F32)
    for i in range(n_sel):
        si = score[i:i + 1, :]
        ahead = (si > score) | ((si == score) & (blk > i))
        rank = rank + jnp.where(ahead, 1.0, 0.0)
    sel_ref[0] = jnp.where((rank < n_top) & valid, 1.0, 0.0).astype(sel_ref.dtype)


def cmp_attention(q, kc, vc, ng, c2s, B, S, n_cmp, n_top, t=512):
    T, W = q.shape
    nq = S // t
    gw = NSA_REP * HEAD_DIM
    G = W // gw
    n_sel = c2s.shape[0]

    def qspec(w, col):
        return pl.BlockSpec((t, w), (lambda b, g, i: (b * nq + i, g)) if col else (lambda b, g, i: (b * nq + i, 0)))

    cspec = pl.BlockSpec((1,) + kc.shape[1:], lambda b, g, i: (b * G + g, 0, 0))
    return pl.pallas_call(
        functools.partial(_cmp_attn_kernel, t=t, n_cmp=n_cmp, n_top=n_top),
        grid=(B, G, nq),
        in_specs=[qspec(gw, True), cspec, cspec, qspec(LANES, False), pl.BlockSpec(c2s.shape, lambda b, g, i: (0, 0))],
        out_specs=[qspec(gw, True), pl.BlockSpec((1, n_sel, t), lambda b, g, i: (g, 0, b * nq + i))],
        out_shape=[jax.ShapeDtypeStruct((T, W), F32), jax.ShapeDtypeStruct((G, n_sel, T), BF16)],
        compiler_params=_params(("parallel", "parallel", "parallel")),
        name="nsa_cmp_attn",
    )(q, kc, vc, ng, c2s)


def _merge_kernel(oa_ref, ob_ref, oc1_ref, oc2_ref, oc3_ref, od_ref, g0_ref, g1_ref, g2_ref, g3_ref, wb_ref, z_ref):
    oc = (oc1_ref[...] + oc2_ref[...] + oc3_ref[...]).astype(BF16)
    branches = (oa_ref[...], ob_ref[...], oc, od_ref[...])
    gates = (g0_ref, g1_ref, g2_ref, g3_ref)
    z = None
    for n in range(N_BRANCH):
        y = _sigmoid(gates[n][...]) * _mm(branches[n], wb_ref[n])
        z = y if z is None else z + y
    z_ref[...] = z.astype(z_ref.dtype)


def merge(outs, mg, wb, tm=512, tn=512):
    T = mg.shape[0]
    D = wb.shape[-1]
    nj = D // tn
    ospec = pl.BlockSpec((tm, BRANCH_WIDTH), lambda i, j: (i, 0))
    gspecs = [pl.BlockSpec((tm, tn), functools.partial(lambda i, j, n: (i, n * nj + j), n=n)) for n in range(N_BRANCH)]
    return pl.pallas_call(
        _merge_kernel,
        grid=(T // tm, nj),
        in_specs=[ospec] * 6 + gspecs + [pl.BlockSpec((N_BRANCH, BRANCH_WIDTH, tn), lambda i, j: (0, 0, j))],
        out_specs=pl.BlockSpec((tm, tn), lambda i, j: (i, j)),
        out_shape=jax.ShapeDtypeStruct((T, D), BF16),
        compiler_params=_params(("parallel", "parallel"), VMEM_LIMIT),
        name="merge",
    )(*outs, mg, mg, mg, mg, wb)


def _topk_rows(s, k, payload=None):
    n = s.shape[0]
    row = lax.broadcasted_iota(I32, s.shape, 0).astype(F32)
    vals, rows, pays = [], [], []
    for _ in range(k):
        m = jnp.max(s, axis=0, keepdims=True)
        am = jnp.min(jnp.where(s == m, row, float(n)), axis=0, keepdims=True)
        hit = row == am
        vals.append(m)
        rows.append(am)
        if payload is not None:
            pays.append(jnp.sum(jnp.where(hit, payload, 0.0), axis=0, keepdims=True))
        s = jnp.where(hit, -jnp.inf, s)
    vals, rows = jnp.concatenate(vals, axis=0), jnp.concatenate(rows, axis=0)
    return (vals, rows) if payload is None else (vals, rows, jnp.concatenate(pays, axis=0))


def _peer_topk_kernel(q_ref, k1_ref, k2_ref, i1_ref, i2_ref, g_ref):
    q = q_ref[...].astype(BF16)
    half = PEER_QDIM // 2
    v1, a1 = _topk_rows(_mm_nt(k1_ref[0], q[:, :half]), PEER_TOPK)
    v2, a2 = _topk_rows(_mm_nt(k2_ref[0], q[:, half:]), PEER_TOPK)
    h8 = PEER_TOPK // 2

    def pairs(x1, x2):
        rows = [x1[0:1] + x2]
        rows += [x1[a:a + 1] + x2[0:h8] for a in range(1, h8)]
        rows.append(x1[h8:] + x2[0:1])
        return jnp.concatenate(rows, axis=0)

    top_s, _, eidx = _topk_rows(pairs(v1, v2), PEER_TOPK, payload=pairs(a1 * float(PEER_KEYS), a2))
    ex = jnp.exp(top_s - top_s[0:1])
    g_ref[0] = ex / jnp.sum(ex, axis=0, keepdims=True)
    e = eidx.astype(I32)
    i1_ref[0] = lax.shift_right_logical(e, int(math.log2(PEER_KEYS)))
    i2_ref[0] = e & (PEER_KEYS - 1)


def peer_topk(q, k1, k2, tm=256):
    T = q.shape[0]
    H = k1.shape[0]
    ospec = pl.BlockSpec((1, PEER_TOPK, tm), lambda i, h: (h, 0, i))
    kspec = pl.BlockSpec((1,) + k1.shape[1:], lambda i, h: (h, 0, 0))
    return pl.pallas_call(
        _peer_topk_kernel,
        grid=(T // tm, H),
        in_specs=[pl.BlockSpec((tm, PEER_QDIM), lambda i, h: (i, h)), kspec, kspec],
        out_specs=[ospec, ospec, ospec],
        out_shape=[jax.ShapeDtypeStruct((H, PEER_TOPK, T), I32), jax.ShapeDtypeStruct((H, PEER_TOPK, T), I32),
                   jax.ShapeDtypeStruct((H, PEER_TOPK, T), F32)],
        compiler_params=_params(("parallel", "parallel")),
        name="peer_topk",
    )(q, k1, k2)


def _peer_gate_kernel(i1_ref, i2_ref, g_ref, o_ref, *, tm, nr):
    i1, i2, g = i1_ref[...], i2_ref[...], g_ref[...]
    n = PEER_KEYS
    g_hi = g.astype(BF16).astype(F32)
    g_lo = g - g_hi
    row = lax.broadcasted_iota(I32, (tm, n, PEER_SLOTS), 1)
    first = i1 == row
    a = jnp.concatenate([jnp.where(first, g_hi, 0.0).astype(BF16), jnp.where(first, g_lo, 0.0).astype(BF16)], axis=1)
    second = jnp.where(i2 == row, 1.0, 0.0).astype(BF16)
    w2 = lax.dot_general(a, second, (((2,), (2,)), ((0,), (0,))), preferred_element_type=F32)
    w = w2[:, :n] + w2[:, n:]
    for ch in range(n // nr):
        o_ref[ch] = w[:, ch * nr:(ch + 1) * nr, :].reshape(tm * nr, n)


def peer_gates(i1, i2, g, nr=SUBLANES, tm=32):
    T = i1.shape[0]
    n = PEER_KEYS
    spec = pl.BlockSpec((tm, 1, PEER_SLOTS), lambda i: (i, 0, 0))
    return pl.pallas_call(
        functools.partial(_peer_gate_kernel, tm=tm, nr=nr),
        grid=(T // tm,),
        in_specs=[spec, spec, spec],
        out_specs=pl.BlockSpec((n // nr, tm * nr, n), lambda i: (0, i, 0)),
        out_shape=jax.ShapeDtypeStruct((n // nr, T * nr, n), F32),
        compiler_params=_params(("parallel",), VMEM_LIMIT),
        name="peer_gates",
    )(i1, i2, g)


def _peer_dense_kernel(h_ref, u_ref, v_ref, w_ref, res_ref, g_ref, o_ref, hn_ref, *, tm, nr):
    @pl.when(pl.program_id(1) == 0)
    def _():
        o_ref[...] = res_ref[...]

    act = _gelu(_mm_nt(h_ref[...], u_ref[...]))
    n = PEER_KEYS
    cols = []
    for r in range(nr):
        w_r = w_ref[pl.ds(r, tm, stride=nr), :]
        cols.append((w_r * act[:, r * n:(r + 1) * n]).astype(BF16))
    o_ref[...] += _mm(jnp.concatenate(cols, axis=1), v_ref[...])

    @pl.when(pl.program_id(1) == pl.num_programs(1) - 1)
    def _():
        hn_ref[...] = _rms(o_ref[...], g_ref[...]).astype(hn_ref.dtype)


def peer_dense(h, u, v, w, res, norm_g, norm_dtype, nr=SUBLANES, tm=512):
    T, D = h.shape
    E = v.shape[0]
    te = nr * PEER_KEYS
    return pl.pallas_call(
        functools.partial(_peer_dense_kernel, tm=tm, nr=nr),
        grid=(T // tm, E // te),
        in_specs=[pl.BlockSpec((tm, D), lambda i, e: (i, 0)),
                  pl.BlockSpec((te, D), lambda i, e: (e, 0)),
                  pl.BlockSpec((te, D), lambda i, e: (e, 0)),
                  pl.BlockSpec((None, tm * nr, PEER_KEYS), lambda i, e: (e, i, 0)),
                  pl.BlockSpec((tm, D), lambda i, e: (i, 0)),
                  pl.BlockSpec((1, D), lambda i, e: (0, 0))],
        out_specs=[pl.BlockSpec((tm, D), lambda i, e: (i, 0)), pl.BlockSpec((tm, D), lambda i, e: (i, 0))],
        out_shape=[jax.ShapeDtypeStruct((T, D), F32), jax.ShapeDtypeStruct((T, D), norm_dtype)],
        compiler_params=_params(("parallel", "arbitrary"), VMEM_LIMIT),
        name="peer_dense",
    )(h, u, v, w, res, norm_g.reshape(1, D))


def _rope_tables(pos, half, reps, width):
    inv = jnp.exp(-math.log(ROPE_THETA) * jnp.arange(half, dtype=F32) / half)
    ang = pos.astype(F32)[:, None] * inv[None, :]
    pad = width - 2 * half
    cos = jnp.concatenate([jnp.cos(ang), jnp.cos(ang), jnp.ones((pos.shape[0], pad), F32)], axis=-1)
    sin = jnp.concatenate([jnp.sin(ang), jnp.sin(ang), jnp.zeros((pos.shape[0], pad), F32)], axis=-1)
    return jnp.tile(cos, (1, reps)), jnp.tile(sin, (1, reps))


def _projection_weight(w_in_l):
    cols = []
    for order in (_PLAIN_ORDER, _REST_ORDER):
        used = 0
        for name in order:
            scale = QK_SCALE if name in ("fq", "sq") else None
            for off, w in _pieces(name):
                piece = w_in_l[:, off:off + w]
                cols.append(piece if scale is None else piece * scale)
                used += w
            end = _SEG[name][0] + _SEG[name][1]
            if end > used:
                cols.append(jnp.zeros((w_in_l.shape[0], end - used), w_in_l.dtype))
                used = end
    off, w = _IN_SRC["mg"]
    cols.append(w_in_l[:, off:off + w])
    return jnp.concatenate(cols, axis=1).astype(BF16)


def _layer(x, h, B, S, p, next_norm, next_dtype):
    T = B * S
    H, G = MIX_HEADS, NSA_KV_HEADS
    pos = jnp.arange(S)

    w_proj = _projection_weight(p["w_in"])
    plain = matmul(h, w_proj, cols=(0, PLAIN_WIDTH), out_dtype=BF16)
    ps = matmul(h, w_proj, cols=(PLAIN_WIDTH, REST_WIDTH))
    mg = matmul(h, w_proj, cols=(PLAIN_WIDTH + REST_WIDTH, _IN_SRC["mg"][1]))

    wuq = p["mla_w_uq"].reshape(MLA_Q_LORA, H, MLA_NOPE + MLA_ROPE)
    r_half = MLA_ROPE // 2
    wuq = jnp.concatenate([wuq[:, :, :MLA_NOPE].reshape(MLA_Q_LORA, -1),
                           wuq[:, :, MLA_NOPE:MLA_NOPE + r_half].reshape(MLA_Q_LORA, -1),
                           wuq[:, :, MLA_NOPE + r_half:].reshape(MLA_Q_LORA, -1)], axis=1).astype(BF16)
    wukv = p["mla_w_ukv"].reshape(MLA_KV_LORA, H, MLA_NOPE + MLA_V)
    wukv = jnp.concatenate([wukv[:, :, :MLA_NOPE].reshape(MLA_KV_LORA, -1),
                            wukv[:, :, MLA_NOPE:].reshape(MLA_KV_LORA, -1)], axis=1).astype(BF16)
    ncos, nsin = _rope_tables(pos, PARTIAL_ROT // 2, H, HEAD_DIM)
    mcos, msin = _rope_tables(pos, r_half, LANES // MLA_ROPE, MLA_ROPE)
    fb = jnp.pad(p["fox_forget_bias"], (0, LANES - H)).reshape(1, LANES)
    t = prep(ps, fb, p["mla_q_norm"].reshape(1, -1), p["mla_kv_norm"].reshape(1, -1), wuq, wukv,
             ncos, nsin, mcos, msin, S)

    fcum = cumsum_tokens(t["lf"], S)
    frow = fcum[:, :H].reshape(B, S, H).transpose(0, 2, 1)
    def cols(*names):
        return tuple(_SEG[n][0] // LANES for n in names)

    o_a = causal_attention(plain, plain, plain, B, S, cols=cols("fq", "fk", "fv"), fcol=fcum, frow=frow)

    o_b = causal_attention(t["qn"], t["kn"], t["mv"], B, S, qr=t["qr"], kr=t["kr"])

    n_half = S // NSA_CMP_STRIDE
    n_cmp = (S - NSA_CMP_LEN) // NSA_CMP_STRIDE + 1
    n_sel = S // NSA_SEL_LEN
    cmp_end = jnp.arange(n_half) * NSA_CMP_STRIDE + NSA_CMP_LEN - 1
    ccos, csin = _rope_tables(cmp_end, PARTIAL_ROT // 2, 1, HEAD_DIM)
    d = jnp.arange(HEAD_DIM)
    csin = jnp.where(d < PARTIAL_ROT // 2, -csin, csin)
    swap = jnp.where(d < PARTIAL_ROT // 2, d + PARTIAL_ROT // 2, d - PARTIAL_ROT // 2)

    def compressed(tok, pos_emb, w1, w2, rotary):
        u = tok.reshape(B, S, G, HEAD_DIM).transpose(0, 2, 1, 3).reshape(B * G, n_half, NSA_CMP_STRIDE * HEAD_DIM)
        pos8 = jnp.pad(pos_emb.reshape(1, -1), ((0, SUBLANES - 1), (0, 0))).astype(BF16)
        w2s = jnp.where(d[None, :] < PARTIAL_ROT, w2[:, swap], 0.0)
        return compress(u, pos8, w1.astype(BF16), w2.astype(BF16), w2s.astype(BF16), ccos, csin, rotary)

    kc = compressed(t["nkc"], p["nsa_cmp_k_pos"], p["nsa_cmp_k_w1"], p["nsa_cmp_k_w2"], True)
    vc = compressed(t["nvc"], p["nsa_cmp_v_pos"], p["nsa_cmp_v_w1"], p["nsa_cmp_v_w2"], False)
    cmp_start = jnp.arange(n_half) * NSA_CMP_STRIDE
    sel_lo = jnp.arange(n_sel) * NSA_SEL_LEN
    cover = jnp.clip(jnp.minimum(cmp_start[:, None] + NSA_CMP_LEN, sel_lo[None, :] + NSA_SEL_LEN)
                     - jnp.maximum(cmp_start[:, None], sel_lo[None, :]), 0, None)
    c2s = (cover.astype(F32) / NSA_CMP_LEN).astype(BF16).T
    emat = (jnp.arange(S)[None, :] // NSA_SEL_LEN == jnp.arange(n_sel)[:, None]).astype(BF16)
    o_cmp, sel = cmp_attention(t["nq"], kc, vc, t["ng"], c2s, B, S, n_cmp, min(NSA_SEL_TOP, n_sel))
    o_slc = nsa_attention(t["nq"], t["nks"], t["nvs"], t["ng"], B, S, mode="select", branch=1, sel=sel, emat=emat)
    o_win = nsa_attention(t["nq"], t["nkw"], t["nvw"], t["ng"], B, S, mode="window", branch=2, window=NSA_WINDOW)

    o_d = stickbreak_attention(plain, plain, plain, B, S, cols=cols("sq", "sk", "sv"))

    z = merge([o_a, o_b, o_cmp, o_slc, o_win, o_d], mg, p["w_branch"].astype(BF16))
    x, h2 = matmul_norm(z, p["w_out"].astype(BF16), x, p["ffn_norm"])

    pq = matmul(h2, p["peer_w_q"].astype(BF16))
    i1, i2, g = peer_topk(pq, p["peer_sub_k1"].astype(BF16), p["peer_sub_k2"].astype(BF16))

    def slots(a):
        return a.reshape(PEER_SLOTS, T).T.reshape(T, 1, PEER_SLOTS)

    w = peer_gates(slots(i1), slots(i2), slots(g))
    return peer_dense(h2, p["peer_u"].astype(BF16), p["peer_v"].astype(BF16), w, x, next_norm, next_dtype)


def kernel(x, attn_norm, w_in, fox_forget_bias, mla_q_norm, mla_kv_norm, mla_w_uq, mla_w_ukv, nsa_cmp_k_pos,
           nsa_cmp_k_w1, nsa_cmp_k_w2, nsa_cmp_v_pos, nsa_cmp_v_w1, nsa_cmp_v_w2, w_branch, w_out, ffn_norm,
           peer_w_q, peer_sub_k1, peer_sub_k2, peer_u, peer_v, final_norm):
    B, S, D = x.shape
    stacked = dict(attn_norm=attn_norm, w_in=w_in, fox_forget_bias=fox_forget_bias, mla_q_norm=mla_q_norm,
                   mla_kv_norm=mla_kv_norm, mla_w_uq=mla_w_uq, mla_w_ukv=mla_w_ukv, nsa_cmp_k_pos=nsa_cmp_k_pos,
                   nsa_cmp_k_w1=nsa_cmp_k_w1, nsa_cmp_k_w2=nsa_cmp_k_w2, nsa_cmp_v_pos=nsa_cmp_v_pos,
                   nsa_cmp_v_w1=nsa_cmp_v_w1, nsa_cmp_v_w2=nsa_cmp_v_w2, w_branch=w_branch, w_out=w_out,
                   ffn_norm=ffn_norm, peer_w_q=peer_w_q, peer_sub_k1=peer_sub_k1, peer_sub_k2=peer_sub_k2,
                   peer_u=peer_u, peer_v=peer_v)
    depth = w_in.shape[0]
    xt = x.reshape(B * S, D)
    h = rmsnorm(xt, attn_norm[0], BF16)
    for l in range(depth):
        last = l == depth - 1
        xt, h = _layer(xt, h, B, S, {k: v[l] for k, v in stacked.items()},
                       final_norm if last else attn_norm[l + 1], F32 if last else BF16)
    return h.reshape(B, S, D)
```
